```python
import math
import jax, jax.numpy as jnp
from jax import lax
import numpy as np

D_MODEL = 2048
BATCH = 4
SEQ = 2048
DEPTH = 4
DEC_BATCH = 128
DEC_SEQ = 8
PAST_LEN = 8192
PAGE_SIZE = 128

MIX_WIDTH = D_MODEL
MLA_HEADS = 8
QK_NOPE_DIM = 128
QK_ROPE_DIM = 64
V_HEAD_DIM = 128
Q_LORA_RANK = 512
KV_LORA_RANK = 512
CONV_DIM = MIX_WIDTH - MLA_HEADS * V_HEAD_DIM
CONV_WIDTH = 3
D_FF = 4 * D_MODEL
ROPE_THETA = 10000.0
NORM_EPS = 1e-6
Q_BLOCK = 128
IN_COLS = Q_LORA_RANK + KV_LORA_RANK + QK_ROPE_DIM + 3 * CONV_DIM
SOFTMAX_SCALE = 1.0 / math.sqrt(QK_NOPE_DIM + QK_ROPE_DIM)

kernel_name = 'hymba_mla_shortconv_decoder_step'


def rms_norm(x, g):
    xf = x.astype(jnp.float32)
    y = xf * lax.rsqrt(jnp.mean(xf * xf, axis=-1, keepdims=True) + NORM_EPS)
    return (y * g.astype(jnp.float32)).astype(x.dtype)


def rope(x, pos):
    half = QK_ROPE_DIM // 2
    inv = ROPE_THETA ** (-jnp.arange(half, dtype=jnp.float32) / half)
    ang = pos.astype(jnp.float32)[:, None] * inv[None, :]
    shape = (1, pos.shape[0]) + (1,) * (x.ndim - 3) + (half,)
    cos = jnp.cos(ang).reshape(shape)
    sin = jnp.sin(ang).reshape(shape)
    xf = x.astype(jnp.float32)
    x1, x2 = xf[..., :half], xf[..., half:]
    return jnp.concatenate([x1 * cos - x2 * sin, x1 * sin + x2 * cos], axis=-1).astype(x.dtype)


def adaln(c, w_ada, b_ada):
    mod = jax.nn.silu(c) @ w_ada + b_ada
    return jnp.split(mod[:, None, :], 6, axis=-1)


def modulate(x, g, shift, scale):
    return rms_norm(x, g) * (1 + scale) + shift


def latent_scores(qa, qp, kc, kp):
    return (jnp.einsum('bqhr,bkr->bhqk', qa, kc) + jnp.einsum('bqhp,bkp->bhqk', qp, kp)) * SOFTMAX_SCALE


def mixer_inputs(h, pos, w_in, g_q_lat, g_kv_lat, w_uq, w_uk):
    proj = h @ w_in
    s1 = Q_LORA_RANK
    s2 = s1 + KV_LORA_RANK
    s3 = s2 + QK_ROPE_DIM
    s4 = s3 + CONV_DIM
    s5 = s4 + CONV_DIM
    q_lat, kv_lat, k_pe, b_gate, c_gate, xv = jnp.split(proj, [s1, s2, s3, s4, s5], axis=-1)
    q = jnp.einsum('btr,rhd->bthd', rms_norm(q_lat, g_q_lat), w_uq)
    q_nope, q_pe = q[..., :QK_NOPE_DIM], q[..., QK_NOPE_DIM:]
    q_abs = jnp.einsum('bthd,rhd->bthr', q_nope, w_uk)
    q_pe = rope(q_pe, pos)
    ckv = rms_norm(kv_lat, g_kv_lat)
    kpe = rope(k_pe, pos)
    u = c_gate * xv
    return q_abs, q_pe, ckv, kpe, b_gate, u


def mla_prompt(q_abs, q_pe, ckv, kpe):
    b, t, h, r = q_abs.shape
    kc = ckv.astype(jnp.float32)
    kp = kpe.astype(jnp.float32)
    key_pos = jnp.arange(t)

    def block(i):
        start = i * Q_BLOCK
        qa = lax.dynamic_slice_in_dim(q_abs, start, Q_BLOCK, axis=1).astype(jnp.float32)
        qp = lax.dynamic_slice_in_dim(q_pe, start, Q_BLOCK, axis=1).astype(jnp.float32)
        s = latent_scores(qa, qp, kc, kp)
        qpos = start + jnp.arange(Q_BLOCK)
        s = jnp.where(key_pos[None, :] <= qpos[:, None], s, -jnp.inf)
        p = jax.nn.softmax(s, axis=-1)
        return jnp.einsum('bhqk,bkr->bqhr', p, kc)

    out = lax.map(block, jnp.arange(t // Q_BLOCK))
    return out.transpose(1, 0, 2, 3, 4).reshape(b, t, h, r).astype(q_abs.dtype)


def mla_sample(q_abs, q_pe, ckv_new, kpe_new, cache_ckv_l, cache_kpe_l, page_table):
    bd, t, h, r = q_abs.shape
    qa = q_abs.astype(jnp.float32)
    qp = q_pe.astype(jnp.float32)

    def page_step(carry, phys):
        m, l, acc = carry
        kc = cache_ckv_l[phys].astype(jnp.float32)
        kp = cache_kpe_l[phys].astype(jnp.float32)
        s = latent_scores(qa, qp, kc, kp)
        m_new = jnp.maximum(m, s.max(axis=-1))
        corr = jnp.exp(m - m_new)
        p = jnp.exp(s - m_new[..., None])
        l = l * corr + p.sum(axis=-1)
        acc = acc * corr[..., None] + jnp.einsum('bhqk,bkr->bhqr', p, kc)
        return (m_new, l, acc), None

    init = (jnp.full((bd, h, t), -jnp.inf, jnp.float32),
            jnp.zeros((bd, h, t), jnp.float32),
            jnp.zeros((bd, h, t, r), jnp.float32))
    (m, l, acc), _ = lax.scan(page_step, init, page_table.T)
    kc = ckv_new.astype(jnp.float32)
    kp = kpe_new.astype(jnp.float32)
    s = latent_scores(qa, qp, kc, kp)
    causal = jnp.arange(t)[None, :] <= jnp.arange(t)[:, None]
    s = jnp.where(causal, s, -jnp.inf)
    m_new = jnp.maximum(m, s.max(axis=-1))
    corr = jnp.exp(m - m_new)
    p = jnp.exp(s - m_new[..., None])
    l = l * corr + p.sum(axis=-1)
    acc = acc * corr[..., None] + jnp.einsum('bhqk,bkr->bhqr', p, kc)
    return (acc / l[..., None]).transpose(0, 2, 1, 3).astype(q_abs.dtype)


def causal_conv(u, prev, conv_w):
    t = u.shape[1]
    padded = jnp.concatenate([prev.astype(u.dtype), u], axis=1)
    y = sum(conv_w[k] * padded[:, k:k + t] for k in range(CONV_WIDTH))
    return y, padded[:, -(CONV_WIDTH - 1):]


def mix_sublayer(x, shift, scale, gate, pos, conv_prev, attend, g_pre, g_post, w_in, g_q_lat,
                 g_kv_lat, w_uq, w_uk, w_uv, conv_w, w_o):
    h = modulate(x, g_pre, shift, scale)
    q_abs, q_pe, ckv, kpe, b_gate, u = mixer_inputs(h, pos, w_in, g_q_lat, g_kv_lat, w_uq, w_uk)
    lat = attend(q_abs, q_pe, ckv, kpe)
    attn = jnp.einsum('bthr,rhd->bthd', lat, w_uv)
    attn = attn.reshape(attn.shape[0], attn.shape[1], MLA_HEADS * V_HEAD_DIM)
    y_conv, conv_state = causal_conv(u, conv_prev, conv_w)
    out = jnp.concatenate([attn, b_gate * y_conv], axis=-1) @ w_o
    x = x + gate * rms_norm(out, g_post)
    return x, ckv, kpe, conv_state


def mlp_sublayer(x, shift, scale, gate, g_pre, g_post, w_up, w_down):
    h = modulate(x, g_pre, shift, scale)
    a = jnp.square(jax.nn.relu(h @ w_up))
    return x + gate * rms_norm(a @ w_down, g_post)


def setup_inputs(seed: int = 0) -> dict:
    key = jax.random.key(seed)
    ks = jax.random.split(key, 32)
    n_pages = PAST_LEN // PAGE_SIZE
    n_used = DEC_BATCH * n_pages
    n_phys = n_used + n_used // 4
    f32 = jnp.float32

    def nrm(k, shape, s):
        return jax.random.normal(k, shape, f32) * s

    def gain(k, shape):
        return 1.0 + 0.02 * jax.random.normal(k, shape, f32)

    page_table = jax.random.permutation(ks[7], n_phys)[:n_used].reshape(DEC_BATCH, n_pages).astype(jnp.int32)
    return {
        'x_prompt': nrm(ks[0], (BATCH, SEQ, D_MODEL), 1.0),
        'x_sample': nrm(ks[1], (DEC_BATCH, DEC_SEQ, D_MODEL), 1.0),
        'c_prompt': nrm(ks[2], (BATCH, D_MODEL), 1.0),
        'c_sample': nrm(ks[3], (DEC_BATCH, D_MODEL), 1.0),
        'cache_ckv': nrm(ks[4], (DEPTH, n_phys, PAGE_SIZE, KV_LORA_RANK), 1.0),
        'cache_kpe': nrm(ks[5], (DEPTH, n_phys, PAGE_SIZE, QK_ROPE_DIM), 1.0),
        'state_conv': nrm(ks[6], (DEPTH, DEC_BATCH, CONV_WIDTH - 1, CONV_DIM), 1.0),
        'page_table': page_table,
        'w_ada': nrm(ks[8], (DEPTH, D_MODEL, 6 * D_MODEL), 0.5 * D_MODEL ** -0.5),
        'b_ada': nrm(ks[9], (DEPTH, 6 * D_MODEL), 0.02),
        'g_pre_mix': gain(ks[10], (DEPTH, D_MODEL)),
        'g_post_mix': gain(ks[11], (DEPTH, D_MODEL)),
        'w_in': nrm(ks[12], (DEPTH, D_MODEL, IN_COLS), D_MODEL ** -0.5),
        'g_q_lat': gain(ks[13], (DEPTH, Q_LORA_RANK)),
        'g_kv_lat': gain(ks[14], (DEPTH, KV_LORA_RANK)),
        'w_uq': nrm(ks[15], (DEPTH, Q_LORA_RANK, MLA_HEADS, QK_NOPE_DIM + QK_ROPE_DIM), Q_LORA_RANK ** -0.5),
        'w_uk': nrm(ks[16], (DEPTH, KV_LORA_RANK, MLA_HEADS, QK_NOPE_DIM), KV_LORA_RANK ** -0.5),
        'w_uv': nrm(ks[17], (DEPTH, KV_LORA_RANK, MLA_HEADS, V_HEAD_DIM), KV_LORA_RANK ** -0.5),
        'conv_w': nrm(ks[18], (DEPTH, CONV_WIDTH, CONV_DIM), CONV_WIDTH ** -0.5),
        'w_o': nrm(ks[19], (DEPTH, MIX_WIDTH, D_MODEL), MIX_WIDTH ** -0.5),
        'g_pre_mlp': gain(ks[20], (DEPTH, D_MODEL)),
        'g_post_mlp': gain(ks[21], (DEPTH, D_MODEL)),
        'w_up': nrm(ks[22], (DEPTH, D_MODEL, D_FF), D_MODEL ** -0.5),
        'w_down': nrm(ks[23], (DEPTH, D_FF, D_MODEL), D_FF ** -0.5),
    }


def reference(x_prompt, x_sample, c_prompt, c_sample, cache_ckv, cache_kpe, state_conv, page_table,
              w_ada, b_ada, g_pre_mix, g_post_mix, w_in, g_q_lat, g_kv_lat, w_uq, w_uk, w_uv, conv_w,
              w_o, g_pre_mlp, g_post_mlp, w_up, w_down):
    pos_p = jnp.arange(x_prompt.shape[1], dtype=jnp.int32)
    pos_s = PAST_LEN + jnp.arange(x_sample.shape[1], dtype=jnp.int32)
    x_p, x_s = x_prompt, x_sample
    ckv_p_l, kpe_p_l, conv_p_l = [], [], []
    ckv_s_l, kpe_s_l, conv_s_l = [], [], []
    for l in range(DEPTH):
        mp = adaln(c_prompt, w_ada[l], b_ada[l])
        ms = adaln(c_sample, w_ada[l], b_ada[l])
        conv0 = jnp.zeros((x_p.shape[0], CONV_WIDTH - 1, CONV_DIM), x_p.dtype)
        x_p, ckv_p, kpe_p, conv_p = mix_sublayer(
            x_p, mp[0], mp[1], mp[2], pos_p, conv0, mla_prompt, g_pre_mix[l], g_post_mix[l], w_in[l],
            g_q_lat[l], g_kv_lat[l], w_uq[l], w_uk[l], w_uv[l], conv_w[l], w_o[l])
        attend_s = lambda qa, qp, kc, kp, li=l: mla_sample(qa, qp, kc, kp, cache_ckv[li], cache_kpe[li], page_table)
        x_s, ckv_s, kpe_s, conv_s = mix_sublayer(
            x_s, ms[0], ms[1], ms[2], pos_s, state_conv[l], attend_s, g_pre_mix[l], g_post_mix[l], w_in[l],
            g_q_lat[l], g_kv_lat[l], w_uq[l], w_uk[l], w_uv[l], conv_w[l], w_o[l])
        x_p = mlp_sublayer(x_p, mp[3], mp[4], mp[5], g_pre_mlp[l], g_post_mlp[l], w_up[l], w_down[l])
        x_s = mlp_sublayer(x_s, ms[3], ms[4], ms[5], g_pre_mlp[l], g_post_mlp[l], w_up[l], w_down[l])
        ckv_p_l.append(ckv_p)
        kpe_p_l.append(kpe_p)
        conv_p_l.append(conv_p)
        ckv_s_l.append(ckv_s)
        kpe_s_l.append(kpe_s)
        conv_s_l.append(conv_s)
    return (x_p, x_s, jnp.stack(ckv_p_l), jnp.stack(kpe_p_l), jnp.stack(conv_p_l),
            jnp.stack(ckv_s_l), jnp.stack(kpe_s_l), jnp.stack(conv_s_l))
```

```python
import functools
import math

import jax
import jax.numpy as jnp
from jax import lax
from jax.experimental import pallas as pl
from jax.experimental.pallas import tpu as pltpu

NORM_EPS = 1e-6
ROPE_THETA = 10000.0
LANE = 128
SUBLANE = 8
VMEM_CAP_BYTES = 60 * 1024 * 1024

F32 = jnp.float32
BF16 = jnp.bfloat16


def _padded_bytes(shape, dtype):
    dims = [d or 1 for d in shape]
    dims[-1] = -(-dims[-1] // LANE) * LANE
    if len(dims) > 1:
        dims[-2] = -(-dims[-2] // SUBLANE) * SUBLANE
    n = jnp.dtype(dtype).itemsize
    for d in dims:
        n *= d
    return n


def _pcall(body, *, name, grid, in_specs, out_specs, out_shape, args, scratch=(), semaphores=(), temp_bytes=0,
           prefetch=()):
    outs = out_shape if isinstance(out_shape, (list, tuple)) else [out_shape]
    ospecs = out_specs if isinstance(out_specs, (list, tuple)) else [out_specs]
    block_bytes = sum(_padded_bytes(sp.block_shape, a.dtype) for sp, a in zip(in_specs, args)
                      if sp.block_shape is not None)
    block_bytes += sum(_padded_bytes(sp.block_shape, o.dtype) for sp, o in zip(ospecs, outs))
    scratch_bytes = sum(_padded_bytes(sh, dt) for sh, dt in scratch)
    vmem = 2 * block_bytes + scratch_bytes + temp_bytes + (4 << 20)
    grid_spec = pltpu.PrefetchScalarGridSpec(
        num_scalar_prefetch=len(prefetch), grid=grid, in_specs=list(in_specs), out_specs=out_specs,
        scratch_shapes=[pltpu.VMEM(sh, dt) for sh, dt in scratch] + list(semaphores))
    return pl.pallas_call(
        body, grid_spec=grid_spec, out_shape=out_shape,
        compiler_params=pltpu.CompilerParams(dimension_semantics=("arbitrary",) * len(grid),
                                             vmem_limit_bytes=int(min(vmem, VMEM_CAP_BYTES))),
        name=name,
    )(*prefetch, *args)


def _dot(a, b):
    return jnp.dot(a, b, preferred_element_type=F32)


def _dot_nt(a, b):
    return lax.dot_general(a, b, (((1,), (1,)), ((), ())), preferred_element_type=F32)


def _rms(x, g):
    ms = jnp.mean(x * x, axis=-1, keepdims=True)
    return x * lax.rsqrt(ms + NORM_EPS) * g


def _rope_rotate(x, cos, sin_signed):
    lane = lax.broadcasted_iota(jnp.int32, x.shape, 1)
    rot = jnp.where(lane < 32, pltpu.roll(x, LANE - 32, 1), pltpu.roll(x, 32, 1))
    return x * cos + rot * sin_signed


def _per_seq(x, mod, fn):
    if mod.ndim == 3:
        return fn(x, mod[0])
    n = mod.shape[0]
    rows, d = x.shape
    return fn(x.reshape(n, rows // n, d), mod[:, None, :]).reshape(rows, d)


def _modulate(x, g, shift, scale):
    y = _per_seq(_rms(x, g), scale, lambda a, sc: a * (1.0 + sc))
    return _per_seq(y, shift, lambda a, sh: a + sh)


def _gated_residual(x, gate, y):
    return x + _per_seq(y, gate, lambda a, gt: a * gt)


def _lane_tile(x, n):
    return jnp.concatenate([x] * n, axis=1) if n > 1 else x


def _adaln_kernel(c_ref, w_ref, b_ref, o_ref):
    c = c_ref[...]
    s = c * (1.0 / (1.0 + jnp.exp(-c)))
    o_ref[...] = _dot(s.astype(BF16), w_ref[...].astype(BF16)) + b_ref[...]


def _adaln(c_all, w_ada, b_ada, tn=1024):
    depth, d, n = w_ada.shape
    r = c_all.shape[0]
    return _pcall(
        _adaln_kernel, name="adaln", grid=(depth, n // tn),
        in_specs=[
            pl.BlockSpec((r, d), lambda l, j: (0, 0)),
            pl.BlockSpec((None, d, tn), lambda l, j: (l, 0, j)),
            pl.BlockSpec((None, 1, tn), lambda l, j: (l, 0, j)),
        ],
        out_specs=pl.BlockSpec((None, r, tn), lambda l, j: (l, 0, j)),
        out_shape=jax.ShapeDtypeStruct((depth, r, n), F32),
        args=(c_all, w_ada, b_ada.reshape(depth, 1, n)),
        temp_bytes=d * tn * 2 + r * (d + tn) * 8,
    )


def _qkv_kernel(x_ref, sh_ref, sc_ref, gpre_ref, wqkv_ref, gq_ref, gkv_ref, wn_ref, wp_ref, wuk_ref,
                cos_ref, sin_ref,
                qa_ref, qp_ref, ckv_ref, kpe_ref, ckvb_ref, kpeb_ref, *, heads, r_q, r_kv, nope, pe, q_scale):
    x = x_ref[...]
    h = _modulate(x, gpre_ref[...], sh_ref[...], sc_ref[...])
    lat = _dot(h.astype(BF16), wqkv_ref[...])
    cos = cos_ref[...]
    sin = sin_ref[...]

    qn = _rms(lat[:, :r_q], gq_ref[...]).astype(BF16)
    q_nope = _dot(qn, wn_ref[...])
    q_pe = _dot(qn, wp_ref[...])
    for hd in range(heads):
        qa = _dot(q_nope[:, hd * nope:(hd + 1) * nope].astype(BF16), wuk_ref[hd]) * q_scale
        qa_ref[hd] = qa.astype(qa_ref.dtype)
        qr = _rope_rotate(q_pe[:, hd * LANE:(hd + 1) * LANE], cos, sin) * q_scale
        qp_ref[hd] = qr[:, :qp_ref.shape[-1]].astype(qp_ref.dtype)

    ckv = _rms(lat[:, r_q:r_q + r_kv], gkv_ref[...])
    ckv_ref[...] = ckv
    ckvb_ref[...] = ckv.astype(BF16)
    kr = _rope_rotate(lat[:, r_q + r_kv:], cos, sin)
    kpe_ref[...] = kr[:, :pe]
    kpeb_ref[...] = kr.astype(BF16)


def _qkv(x2d, shift, scale, g_pre, w_qkv, g_q, g_kv, w_nope, w_pe, w_uk_t, cos_t, sin_t, *,
         tm, mod_block, mod_index, rope_index, q_dtype, qp_width, q_scale):
    m, d = x2d.shape
    heads, nope, r_kv = w_uk_t.shape
    r_q = g_q.shape[-1]
    pe = 64
    nl = w_qkv.shape[1]
    kern = functools.partial(_qkv_kernel, heads=heads, r_q=r_q, r_kv=r_kv, nope=nope, pe=pe, q_scale=q_scale)
    const2 = lambda i: (0, 0)
    return _pcall(
        kern, name="qkv", grid=(m // tm,),
        in_specs=[
            pl.BlockSpec((tm, d), lambda i: (i, 0)),
            pl.BlockSpec(mod_block, mod_index),
            pl.BlockSpec(mod_block, mod_index),
            pl.BlockSpec((1, d), const2),
            pl.BlockSpec((d, nl), const2),
            pl.BlockSpec((1, r_q), const2),
            pl.BlockSpec((1, r_kv), const2),
            pl.BlockSpec(w_nope.shape, const2),
            pl.BlockSpec(w_pe.shape, const2),
            pl.BlockSpec(w_uk_t.shape, lambda i: (0, 0, 0)),
            pl.BlockSpec((tm, LANE), rope_index),
            pl.BlockSpec((tm, LANE), rope_index),
        ],
        out_specs=[
            pl.BlockSpec((heads, tm, r_kv), lambda i: (0, i, 0)),
            pl.BlockSpec((heads, tm, qp_width), lambda i: (0, i, 0)),
            pl.BlockSpec((tm, r_kv), lambda i: (i, 0)),
            pl.BlockSpec((tm, pe), lambda i: (i, 0)),
            pl.BlockSpec((tm, r_kv), lambda i: (i, 0)),
            pl.BlockSpec((tm, LANE), lambda i: (i, 0)),
        ],
        out_shape=[
            jax.ShapeDtypeStruct((heads, m, r_kv), q_dtype),
            jax.ShapeDtypeStruct((heads, m, qp_width), q_dtype),
            jax.ShapeDtypeStruct((m, r_kv), F32),
            jax.ShapeDtypeStruct((m, pe), F32),
            jax.ShapeDtypeStruct((m, r_kv), BF16),
            jax.ShapeDtypeStruct((m, LANE), BF16),
        ],
        args=(x2d, shift, scale, g_pre, w_qkv, g_q, g_kv, w_nope, w_pe, w_uk_t, cos_t, sin_t),
        temp_bytes=tm * (2 * d + nl + heads * (nope + LANE) + 2 * r_kv) * 4,
    )


def _gate_kernel(*refs, seq_rows, tiles_per_seq, has_prev):
    if has_prev:
        (x_ref, sh_ref, sc_ref, gpre_ref, wb_ref, wc_ref, wx_ref, cw_ref, p0_ref, p1_ref,
         gc_ref, u_ref, h_sc, carry_sc) = refs
    else:
        (x_ref, sh_ref, sc_ref, gpre_ref, wb_ref, wc_ref, wx_ref, cw_ref,
         gc_ref, u_ref, h_sc, carry_sc) = refs
    i = pl.program_id(0)
    c = pl.program_id(1)

    @pl.when(c == 0)
    def _():
        h = _modulate(x_ref[...], gpre_ref[...], sh_ref[...], sc_ref[...])
        h_sc[...] = h.astype(BF16)

    if not has_prev:
        @pl.when(i % tiles_per_seq == 0)
        def _():
            carry_sc[c] = jnp.zeros(carry_sc.shape[1:], F32)

    h = h_sc[...]
    b_gate = _dot(h, wb_ref[...])
    u = _dot(h, wc_ref[...]) * _dot(h, wx_ref[...])
    tm = u.shape[0]
    row = lax.broadcasted_iota(jnp.int32, u.shape, 0)
    if has_prev:
        pos = row % seq_rows
        p0 = p0_ref[...]
        p1 = p1_ref[...]
    else:
        pos = row
        prev = carry_sc[c]
        p0 = prev[SUBLANE - 2:SUBLANE - 1, :]
        p1 = prev[SUBLANE - 1:SUBLANE, :]
    u1 = jnp.where(pos == 0, p1, pltpu.roll(u, 1, 0))
    u2 = jnp.where(pos == 0, p0, jnp.where(pos == 1, p1, pltpu.roll(u, 2, 0)))
    cw = cw_ref[...]
    y = cw[0:1, :] * u2 + cw[1:2, :] * u1 + cw[2:3, :] * u
    gc_ref[...] = (b_gate * y).astype(gc_ref.dtype)
    if has_prev:
        u_ref[...] = u
    else:
        tail = u[tm - SUBLANE:, :]
        carry_sc[c] = tail
        u_ref[...] = tail


def _gate(x2d, shift, scale, g_pre, w_gate, conv_w, prev0, prev1, *, tm, tc, mod_block, mod_index, seq_rows):
    m, d = x2d.shape
    cdim = conv_w.shape[-1]
    nc = cdim // tc
    has_prev = prev0 is not None
    tiles_per_seq = max(seq_rows // tm, 1)
    kern = functools.partial(_gate_kernel, seq_rows=seq_rows, tiles_per_seq=tiles_per_seq,
                             has_prev=has_prev)
    in_specs = [
        pl.BlockSpec((tm, d), lambda i, c: (i, 0)),
        pl.BlockSpec(mod_block, lambda i, c: mod_index(i)),
        pl.BlockSpec(mod_block, lambda i, c: mod_index(i)),
        pl.BlockSpec((1, d), lambda i, c: (0, 0)),
        pl.BlockSpec((d, tc), lambda i, c: (0, c)),
        pl.BlockSpec((d, tc), lambda i, c: (0, nc + c)),
        pl.BlockSpec((d, tc), lambda i, c: (0, 2 * nc + c)),
        pl.BlockSpec((3, tc), lambda i, c: (0, c)),
    ]
    args = [x2d, shift, scale, g_pre, w_gate, w_gate, w_gate, conv_w]
    if has_prev:
        in_specs += [pl.BlockSpec((tm, tc), lambda i, c: (i, c))] * 2
        args += [prev0, prev1]
        u_spec = pl.BlockSpec((tm, tc), lambda i, c: (i, c))
        u_shape = jax.ShapeDtypeStruct((m, cdim), F32)
    else:
        u_spec = pl.BlockSpec((None, SUBLANE, tc), lambda i, c: (i, 0, c))
        u_shape = jax.ShapeDtypeStruct((m // tm, SUBLANE, cdim), F32)
    return _pcall(
        kern, name="gate_conv", grid=(m // tm, nc),
        in_specs=in_specs,
        out_specs=[pl.BlockSpec((tm, tc), lambda i, c: (i, c)), u_spec],
        out_shape=[jax.ShapeDtypeStruct((m, cdim), BF16), u_shape],
        args=args,
        scratch=[((tm, d), BF16), ((nc, SUBLANE, tc), F32)],
        temp_bytes=tm * d * 8 + tm * tc * 4 * 10,
    )


def _flash_kernel(qa_ref, qp_ref, kc_ref, kp_ref, wuv_ref, o_ref, m_sc, l_sc, acc_sc, *, tq, tk, heads, groups):
    qi = pl.program_id(1)
    hpg = heads // groups
    gr = hpg * tq
    nr = acc_sc.shape[-1] // LANE
    m_sc[...] = jnp.full(m_sc.shape, -jnp.inf, F32)
    l_sc[...] = jnp.zeros(l_sc.shape, F32)
    acc_sc[...] = jnp.zeros(acc_sc.shape, F32)

    def tile(j, masked):
        k0 = pl.multiple_of(j * tk, tk)
        kc = kc_ref[pl.ds(k0, tk), :]
        kp = kp_ref[pl.ds(k0, tk), :]
        for g in range(groups):
            rs = slice(g * gr, (g + 1) * gr)
            qa = qa_ref[g * hpg:(g + 1) * hpg].reshape(gr, qa_ref.shape[-1])
            qp = qp_ref[g * hpg:(g + 1) * hpg].reshape(gr, qp_ref.shape[-1])
            s = _dot_nt(qa, kc) + _dot_nt(qp, kp)
            if masked:
                qpos = lax.broadcasted_iota(jnp.int32, s.shape, 0) % tq
                kpos = lax.broadcasted_iota(jnp.int32, s.shape, 1)
                s = jnp.where(kpos - qpos <= qi * tq - j * tk, s, -jnp.inf)
            m_prev = m_sc[rs, :]
            m_new = jnp.maximum(m_prev, jnp.max(s, axis=-1, keepdims=True))
            alpha = jnp.exp2(m_prev - m_new)
            p = jnp.exp2(s - _lane_tile(m_new, tk // LANE))
            l_sc[rs, :] = alpha * l_sc[rs, :] + jnp.sum(p, axis=-1, keepdims=True)
            acc_sc[rs, :] = _lane_tile(alpha, nr) * acc_sc[rs, :] + _dot(p.astype(BF16), kc)
            m_sc[rs, :] = m_new

    n_full = (qi * tq) // tk

    def body(j, carry):
        tile(j, False)
        return carry

    lax.fori_loop(0, n_full, body, 0)
    tile(n_full, True)

    inv = _lane_tile(1.0 / l_sc[...], nr)
    dv = wuv_ref.shape[-1]
    for hd in range(heads):
        lat = (acc_sc[hd * tq:(hd + 1) * tq, :] * inv[hd * tq:(hd + 1) * tq, :]).astype(BF16)
        o_ref[:, hd * dv:(hd + 1) * dv] = _dot(lat, wuv_ref[hd]).astype(o_ref.dtype)


def _flash(qa, qp, ckv_b, kpe_b, w_uv_t, *, batch, seq, tq, tk, groups):
    heads, m, r = qa.shape
    dv = w_uv_t.shape[-1]
    nq = seq // tq
    rows = heads * tq
    assert tk % tq == 0 and seq % tk == 0 and heads % groups == 0
    kern = functools.partial(_flash_kernel, tq=tq, tk=tk, heads=heads, groups=groups)
    return _pcall(
        kern, name="flash_prompt", grid=(batch, nq),
        in_specs=[
            pl.BlockSpec((heads, tq, r), lambda b, q: (0, b * nq + q, 0)),
            pl.BlockSpec((heads, tq, LANE), lambda b, q: (0, b * nq + q, 0)),
            pl.BlockSpec((seq, r), lambda b, q: (b, 0)),
            pl.BlockSpec((seq, LANE), lambda b, q: (b, 0)),
            pl.BlockSpec(w_uv_t.shape, lambda b, q: (0, 0, 0)),
        ],
        out_specs=pl.BlockSpec((tq, heads * dv), lambda b, q: (b * nq + q, 0)),
        out_shape=jax.ShapeDtypeStruct((m, heads * dv), BF16),
        args=(qa, qp, ckv_b, kpe_b, w_uv_t),
        scratch=[((rows, LANE), F32), ((rows, LANE), F32), ((rows, r), F32)],
        temp_bytes=rows * tk * 4 * 4 + rows * r * 4,
    )


def _paged_kernel(pt_ref, qa_ref, qp_ref, cn_ref, kn_ref, ckv_hbm, kpe_hbm, o_ref,
                  m_sc, l_sc, acc_sc, kbuf, pbuf, sems, *, layer, pages, heads, t_new, nbuf, steps_per_seq):
    s = pl.program_id(0)
    total = pl.num_programs(0)
    g = s % steps_per_seq
    depth = nbuf - 1
    rows = heads * t_new
    nr = kbuf.shape[3] // LANE

    def copies(step, slot, p):
        pg = pt_ref[step * pages + p]
        return (pltpu.make_async_copy(ckv_hbm.at[layer, pg], kbuf.at[slot, p], sems.at[0, slot]),
                pltpu.make_async_copy(kpe_hbm.at[layer, pg], pbuf.at[slot, p], sems.at[1, slot]))

    @pl.when(s == 0)
    def _():
        for d in range(depth):
            for p in range(pages):
                for c in copies(d, d, p):
                    c.start()

    @pl.when(g == 0)
    def _():
        m_sc[...] = jnp.full(m_sc.shape, -jnp.inf, F32)
        l_sc[...] = jnp.zeros(l_sc.shape, F32)
        acc_sc[...] = jnp.zeros(acc_sc.shape, F32)

    slot = s % nbuf
    for p in range(pages):
        for c in copies(s, slot, p):
            c.wait()

    qa = qa_ref[...].reshape(rows, qa_ref.shape[-1]).astype(BF16)
    qp = qp_ref[...].reshape(rows, qp_ref.shape[-1]).astype(BF16)

    def merge(m_loc, l_loc, pv):
        m_prev = m_sc[...]
        m_new = jnp.maximum(m_prev, m_loc)
        alpha = jnp.exp2(m_prev - m_new)
        w = jnp.exp2(m_loc - m_new)
        l_sc[...] = alpha * l_sc[...] + w * l_loc
        acc_sc[...] = _lane_tile(alpha, nr) * acc_sc[...] + _lane_tile(w, nr) * pv
        m_sc[...] = m_new

    nxt = s + depth
    nxt_slot = nxt % nbuf
    nxt = jnp.where(nxt >= total, nxt - total, nxt)

    def pair_k(i):
        return jnp.concatenate([kbuf[slot, i].astype(BF16), kbuf[slot, i + 1].astype(BF16)], axis=0)

    def pair_kp(i):
        return jnp.concatenate([pbuf[slot, i].astype(BF16), pbuf[slot, i + 1].astype(BF16)], axis=1)

    s_pairs = []
    for i in range(0, pages, 2):
        s_pairs.append(_dot_nt(qa, pair_k(i)) + _dot(qp, pair_kp(i)))
        for j in (i, i + 1):
            for c in copies(nxt, nxt_slot, j):
                c.start()
    m_loc = s_pairs[0].max(axis=-1, keepdims=True)
    for sp in s_pairs[1:]:
        m_loc = jnp.maximum(m_loc, sp.max(axis=-1, keepdims=True))
    l_loc = None
    pv = None
    for n, i in enumerate(range(0, pages, 2)):
        pr = jnp.exp2(s_pairs[n] - m_loc)
        ls = jnp.sum(pr, axis=-1, keepdims=True)
        d = _dot(pr.astype(BF16), pair_k(i))
        l_loc = ls if l_loc is None else l_loc + ls
        pv = d if pv is None else pv + d
    merge(m_loc, l_loc, pv)

    @pl.when(g == steps_per_seq - 1)
    def _():
        kc = cn_ref[...].astype(BF16)
        kp = kn_ref[...].astype(BF16)
        sc = _dot_nt(qa, kc) + _dot_nt(qp, kp)
        tq = lax.broadcasted_iota(jnp.int32, sc.shape, 0) % t_new
        tk = lax.broadcasted_iota(jnp.int32, sc.shape, 1)
        sc = jnp.where(tk <= tq, sc, -jnp.inf)
        m_new = jnp.max(sc, axis=-1, keepdims=True)
        pr = jnp.exp2(sc - m_new)
        merge(m_new, jnp.sum(pr, axis=-1, keepdims=True), _dot(pr.astype(BF16), kc))
        o_ref[...] = (acc_sc[...] / _lane_tile(l_sc[...], nr)).reshape(o_ref.shape)

    @pl.when(s == total - 1)
    def _():
        for d in range(depth):
            for p in range(pages):
                for c in copies(d, (total + d) % nbuf, p):
                    c.wait()


def _paged(qa, qp, cache_ckv, cache_kpe_t, layer, page_table, ckv_new, kpe_new, *, pages, nbuf):
    heads, m, r = qa.shape
    pe = qp.shape[-1]
    nb, n_pages = page_table.shape
    t_new = m // nb
    page = cache_ckv.shape[2]
    rows = heads * t_new
    sps = n_pages // pages
    assert n_pages % pages == 0 and pages % 2 == 0 and nb * sps >= nbuf
    kern = functools.partial(_paged_kernel, layer=layer, pages=pages, heads=heads, t_new=t_new,
                             nbuf=nbuf, steps_per_seq=sps)
    return _pcall(
        kern, name="paged_sample", grid=(nb * sps,),
        in_specs=[pl.BlockSpec((heads, t_new, r), lambda s, pt: (0, s // sps, 0)),
                  pl.BlockSpec((heads, t_new, pe), lambda s, pt: (0, s // sps, 0)),
                  pl.BlockSpec((t_new, r), lambda s, pt: (s // sps, 0)),
                  pl.BlockSpec((t_new, pe), lambda s, pt: (s // sps, 0)),
                  pl.BlockSpec(memory_space=pl.ANY),
                  pl.BlockSpec(memory_space=pl.ANY)],
        out_specs=pl.BlockSpec((heads, t_new, r), lambda s, pt: (0, s // sps, 0)),
        out_shape=jax.ShapeDtypeStruct((heads, m, r), F32),
        args=(qa, qp, ckv_new, kpe_new, cache_ckv, cache_kpe_t),
        scratch=[((rows, LANE), F32), ((rows, LANE), F32), ((rows, r), F32),
                 ((nbuf, pages, page, r), F32), ((nbuf, pages, pe, page), F32)],
        semaphores=[pltpu.SemaphoreType.DMA((2, nbuf))],
        temp_bytes=rows * pages * page * 4 * 4 + pages * page * (r + pe) * 2,
        prefetch=(page_table.reshape(-1),),
    )


def _uv_kernel(lat_ref, w_ref, o_ref):
    o_ref[...] = _dot(lat_ref[...].astype(BF16), w_ref[...]).astype(o_ref.dtype)


def _uv_proj(lat, w_uv_t):
    heads, m, r = lat.shape
    dv = w_uv_t.shape[-1]
    return _pcall(
        _uv_kernel, name="uv_proj", grid=(heads,),
        in_specs=[pl.BlockSpec((None, m, r), lambda h: (h, 0, 0)),
                  pl.BlockSpec((None, r, dv), lambda h: (h, 0, 0))],
        out_specs=pl.BlockSpec((m, dv), lambda h: (0, h)),
        out_shape=jax.ShapeDtypeStruct((m, heads * dv), BF16),
        args=(lat, w_uv_t),
        temp_bytes=m * r * 2 + m * dv * 4,
    )


def _out_kernel(a_ref, gc_ref, wa_ref, wc_ref, x_ref, gate_ref, gpost_ref, o_ref):
    out = _dot(a_ref[...], wa_ref[...]) + _dot(gc_ref[...], wc_ref[...])
    o_ref[...] = _gated_residual(x_ref[...], gate_ref[...], _rms(out, gpost_ref[...]))


def _out_proj(attn, gconv, w_o_a, w_o_c, x2d, gate, g_post, *, tm, mod_block, mod_index):
    m, d = x2d.shape
    ka, kc = attn.shape[1], gconv.shape[1]
    return _pcall(
        _out_kernel, name="out_proj", grid=(m // tm,),
        in_specs=[
            pl.BlockSpec((tm, ka), lambda i: (i, 0)),
            pl.BlockSpec((tm, kc), lambda i: (i, 0)),
            pl.BlockSpec((ka, d), lambda i: (0, 0)),
            pl.BlockSpec((kc, d), lambda i: (0, 0)),
            pl.BlockSpec((tm, d), lambda i: (i, 0)),
            pl.BlockSpec(mod_block, mod_index),
            pl.BlockSpec((1, d), lambda i: (0, 0)),
        ],
        out_specs=pl.BlockSpec((tm, d), lambda i: (i, 0)),
        out_shape=jax.ShapeDtypeStruct((m, d), F32),
        args=(attn, gconv, w_o_a, w_o_c, x2d, gate, g_post),
        temp_bytes=tm * d * 4 * 3,
    )


def _mlp_kernel(x_ref, sh_ref, sc_ref, gate_ref, gpre_ref, gpost_ref, wu_ref, wd_ref, o_ref, h_sc, acc_sc):
    f = pl.program_id(1)

    @pl.when(f == 0)
    def _():
        h = _modulate(x_ref[...], gpre_ref[...], sh_ref[...], sc_ref[...])
        h_sc[...] = h.astype(BF16)
        acc_sc[...] = jnp.zeros(acc_sc.shape, F32)

    a = jnp.maximum(_dot(h_sc[...], wu_ref[...]), 0.0)
    acc_sc[...] += _dot((a * a).astype(BF16), wd_ref[...])

    @pl.when(f == pl.num_programs(1) - 1)
    def _():
        o_ref[...] = _gated_residual(x_ref[...], gate_ref[...], _rms(acc_sc[...], gpost_ref[...]))


def _mlp(x2d, shift, scale, gate, g_pre, g_post, w_up, w_down, *, tm, tf, mod_block, mod_index):
    m, d = x2d.shape
    ff = w_up.shape[1]
    mi = lambda i, f: mod_index(i)
    return _pcall(
        _mlp_kernel, name="mlp", grid=(m // tm, ff // tf),
        in_specs=[
            pl.BlockSpec((tm, d), lambda i, f: (i, 0)),
            pl.BlockSpec(mod_block, mi),
            pl.BlockSpec(mod_block, mi),
            pl.BlockSpec(mod_block, mi),
            pl.BlockSpec((1, d), lambda i, f: (0, 0)),
            pl.BlockSpec((1, d), lambda i, f: (0, 0)),
            pl.BlockSpec((d, tf), lambda i, f: (0, f)),
            pl.BlockSpec((tf, d), lambda i, f: (f, 0)),
        ],
        out_specs=pl.BlockSpec((tm, d), lambda i, f: (i, 0)),
        out_shape=jax.ShapeDtypeStruct((m, d), F32),
        args=(x2d, shift, scale, gate, g_pre, g_post, w_up, w_down),
        scratch=[((tm, d), BF16), ((tm, d), F32)],
        temp_bytes=tm * tf * 4 * 3 + tm * d * 4 * 2,
    )


def _rope_tables(pos):
    half = 32
    inv = ROPE_THETA ** (-jnp.arange(half, dtype=F32) / half)
    ang = pos.astype(F32)[:, None] * inv[None, :]
    cos, sin = jnp.cos(ang), jnp.sin(ang)
    zero = jnp.zeros((pos.shape[0], LANE - 2 * half), F32)
    return (jnp.concatenate([cos, cos, zero], axis=-1),
            jnp.concatenate([-sin, sin, zero], axis=-1))


def kernel(x_prompt, x_sample, c_prompt, c_sample, cache_ckv, cache_kpe, state_conv, page_table, w_ada, b_ada,
           g_pre_mix, g_post_mix, w_in, g_q_lat, g_kv_lat, w_uq, w_uk, w_uv, conv_w, w_o, g_pre_mlp, g_post_mlp,
           w_up, w_down):
    batch, seq, d = x_prompt.shape
    nb, t_new, _ = x_sample.shape
    depth = w_in.shape[0]
    r_q, heads, qk = w_uq.shape[1:]
    r_kv, _, nope = w_uk.shape[1:]
    dv = w_uv.shape[-1]
    pe = qk - nope
    cdim = conv_w.shape[-1]
    past = page_table.shape[1] * cache_ckv.shape[2]
    q_scale = math.log2(math.e) / math.sqrt(qk)
    mp_rows, ms_rows = batch * seq, nb * t_new

    tm_p, tm_s = 512, 512
    tps = seq // tm_p

    n_c = batch + nb
    pad = (-n_c) % SUBLANE
    c_all = jnp.concatenate([c_prompt, c_sample, jnp.zeros((pad, d), F32)], axis=0)
    mod = _adaln(c_all, w_ada, b_ada)

    cos_p, sin_p = _rope_tables(jnp.arange(seq, dtype=jnp.int32))
    cos_s, sin_s = _rope_tables(past + jnp.arange(t_new, dtype=jnp.int32))
    cos_s = jnp.tile(cos_s, (tm_s // t_new, 1))
    sin_s = jnp.tile(sin_s, (tm_s // t_new, 1))

    mod_index_p = lambda i: (i // tps, 0, 0)
    mod_index_s = lambda i: (i, 0)
    rope_index_p = lambda i: (i % tps, 0)
    rope_index_s = lambda i: (0, 0)
    kw_p = dict(tm=tm_p, mod_block=(1, 1, d), mod_index=mod_index_p)
    kw_s = dict(tm=tm_s, mod_block=(tm_s // t_new, d), mod_index=mod_index_s)

    cache_kpe_t = jnp.swapaxes(cache_kpe, 2, 3)

    x_p = x_prompt.reshape(mp_rows, d)
    x_s = x_sample.reshape(ms_rows, d)
    outs = [[] for _ in range(6)]
    for l in range(depth):
        s3 = r_q + r_kv + pe
        w_qkv = jnp.concatenate([w_in[l][:, :s3], jnp.zeros((d, LANE - pe), F32)], axis=1).astype(BF16)
        w_gate = w_in[l][:, s3:].astype(BF16)
        w_nope = w_uq[l][:, :, :nope].reshape(r_q, heads * nope).astype(BF16)
        w_pe = jnp.concatenate([w_uq[l][:, :, nope:], jnp.zeros((r_q, heads, LANE - pe), F32)],
                               axis=-1).reshape(r_q, heads * LANE).astype(BF16)
        w_uk_t = jnp.transpose(w_uk[l], (1, 2, 0)).astype(BF16)
        w_uv_t = jnp.transpose(w_uv[l], (1, 0, 2)).astype(BF16)
        w_o_a = w_o[l][:heads * dv].astype(BF16)
        w_o_c = w_o[l][heads * dv:].astype(BF16)
        w_up_b = w_up[l].astype(BF16)
        w_down_b = w_down[l].astype(BF16)
        g_pre = g_pre_mix[l].reshape(1, d)
        g_post = g_post_mix[l].reshape(1, d)
        g_q = g_q_lat[l].reshape(1, r_q)
        g_kv = g_kv_lat[l].reshape(1, r_kv)
        g_pre2 = g_pre_mlp[l].reshape(1, d)
        g_post2 = g_post_mlp[l].reshape(1, d)

        mods_p = [mod[l, :batch, k * d:(k + 1) * d].reshape(batch, 1, d) for k in range(6)]
        mods_s = [mod[l, batch:batch + nb, k * d:(k + 1) * d] for k in range(6)]

        qa, qp, ckv_p, kpe_p, ckv_pb, kpe_pb = _qkv(
            x_p, mods_p[0], mods_p[1], g_pre, w_qkv, g_q, g_kv, w_nope, w_pe, w_uk_t, cos_p, sin_p,
            rope_index=rope_index_p, q_dtype=BF16, qp_width=LANE, q_scale=q_scale, **kw_p)
        gconv_p, tail_p = _gate(x_p, mods_p[0], mods_p[1], g_pre, w_gate, conv_w[l], None, None,
                                tc=512, seq_rows=seq, **kw_p)
        attn_p = _flash(qa, qp, ckv_pb, kpe_pb, w_uv_t, batch=batch, seq=seq, tq=256, tk=512, groups=2)
        x_p = _out_proj(attn_p, gconv_p, w_o_a, w_o_c, x_p, mods_p[2], g_post, **kw_p)
        x_p = _mlp(x_p, mods_p[3], mods_p[4], mods_p[5], g_pre2, g_post2, w_up_b, w_down_b, tf=1024, **kw_p)

        qa, qp, ckv_s, kpe_s, _, _ = _qkv(
            x_s, mods_s[0], mods_s[1], g_pre, w_qkv, g_q, g_kv, w_nope, w_pe, w_uk_t, cos_s, sin_s,
            rope_index=rope_index_s, q_dtype=F32, qp_width=pe, q_scale=q_scale, **kw_s)
        prev0 = jnp.repeat(state_conv[l][:, 0, :], t_new, axis=0)
        prev1 = jnp.repeat(state_conv[l][:, 1, :], t_new, axis=0)
        gconv_s, u_s = _gate(x_s, mods_s[0], mods_s[1], g_pre, w_gate, conv_w[l], prev0, prev1,
                             tc=512, seq_rows=t_new, **kw_s)
        lat_s = _paged(qa, qp, cache_ckv, cache_kpe_t, l, page_table, ckv_s, kpe_s, pages=16, nbuf=3)
        attn_s = _uv_proj(lat_s, w_uv_t)
        x_s = _out_proj(attn_s, gconv_s, w_o_a, w_o_c, x_s, mods_s[2], g_post, **kw_s)
        x_s = _mlp(x_s, mods_s[3], mods_s[4], mods_s[5], g_pre2, g_post2, w_up_b, w_down_b, tf=1024, **kw_s)

        outs[0].append(ckv_p.reshape(batch, seq, r_kv))
        outs[1].append(kpe_p.reshape(batch, seq, pe))
        outs[2].append(tail_p[tps - 1::tps, SUBLANE - 2:, :])
        outs[3].append(ckv_s.reshape(nb, t_new, r_kv))
        outs[4].append(kpe_s.reshape(nb, t_new, pe))
        outs[5].append(u_s.reshape(nb, t_new, cdim)[:, t_new - 2:, :])

    return (x_p.reshape(batch, seq, d), x_s.reshape(nb, t_new, d),
            jnp.stack(outs[0]), jnp.stack(outs[1]), jnp.stack(outs[2]),
            jnp.stack(outs[3]), jnp.stack(outs[4]), jnp.stack(outs[5]))
```

```python
import functools
import math

import jax
import jax.numpy as jnp
from jax import lax
from jax.experimental import pallas as pl
from jax.experimental.pallas import tpu as pltpu

NORM_EPS = 1e-6
ROPE_THETA = 10000.0
LANE = 128
SUBLANE = 8
VMEM_CAP_BYTES = 60 * 1024 * 1024

F32 = jnp.float32
BF16 = jnp.bfloat16


def _padded_bytes(shape, dtype):
    dims = [d or 1 for d in shape]
    dims[-1] = -(-dims[-1] // LANE) * LANE
    if len(dims) > 1:
        dims[-2] = -(-dims[-2] // SUBLANE) * SUBLANE
    n = jnp.dtype(dtype).itemsize
    for d in dims:
        n *= d
    return n


def _pcall(body, *, name, grid, in_specs, out_specs, out_shape, args, scratch=(), semaphores=(), temp_bytes=0,
           prefetch=()):
    outs = out_shape if isinstance(out_shape, (list, tuple)) else [out_shape]
    ospecs = out_specs if isinstance(out_specs, (list, tuple)) else [out_specs]
    block_bytes = sum(_padded_bytes(sp.block_shape, a.dtype) for sp, a in zip(in_specs, args)
                      if sp.block_shape is not None)
    block_bytes += sum(_padded_bytes(sp.block_shape, o.dtype) for sp, o in zip(ospecs, outs))
    scratch_bytes = sum(_padded_bytes(sh, dt) for sh, dt in scratch)
    vmem = 2 * block_bytes + scratch_bytes + temp_bytes + (4 << 20)
    grid_spec = pltpu.PrefetchScalarGridSpec(
        num_scalar_prefetch=len(prefetch), grid=grid, in_specs=list(in_specs), out_specs=out_specs,
        scratch_shapes=[pltpu.VMEM(sh, dt) for sh, dt in scratch] + list(semaphores))
    return pl.pallas_call(
        body, grid_spec=grid_spec, out_shape=out_shape,
        compiler_params=pltpu.CompilerParams(dimension_semantics=("arbitrary",) * len(grid),
                                             vmem_limit_bytes=int(min(vmem, VMEM_CAP_BYTES))),
        name=name,
    )(*prefetch, *args)


def _dot(a, b):
    return jnp.dot(a, b, preferred_element_type=F32)


def _dot_nt(a, b):
    return lax.dot_general(a, b, (((1,), (1,)), ((), ())), preferred_element_type=F32)


def _rms(x, g):
    ms = jnp.mean(x * x, axis=-1, keepdims=True)
    return x * lax.rsqrt(ms + NORM_EPS) * g


def _rope_rotate(x, cos, sin_signed):
    lane = lax.broadcasted_iota(jnp.int32, x.shape, 1)
    rot = jnp.where(lane < 32, pltpu.roll(x, LANE - 32, 1), pltpu.roll(x, 32, 1))
    return x * cos + rot * sin_signed


def _per_seq(x, mod, fn):
    if mod.ndim == 3:
        return fn(x, mod[0])
    n = mod.shape[0]
    rows, d = x.shape
    return fn(x.reshape(n, rows // n, d), mod[:, None, :]).reshape(rows, d)


def _modulate(x, g, shift, scale):
    y = _per_seq(_rms(x, g), scale, lambda a, sc: a * (1.0 + sc))
    return _per_seq(y, shift, lambda a, sh: a + sh)


def _gated_residual(x, gate, y):
    return x + _per_seq(y, gate, lambda a, gt: a * gt)


def _lane_tile(x, n):
    return jnp.concatenate([x] * n, axis=1) if n > 1 else x


def _adaln_kernel(c_ref, w_ref, b_ref, o_ref):
    c = c_ref[...]
    s = c * (1.0 / (1.0 + jnp.exp(-c)))
    o_ref[...] = _dot(s.astype(BF16), w_ref[...].astype(BF16)) + b_ref[...]


def _adaln(c_all, w_ada, b_ada, tn=1024):
    depth, d, n = w_ada.shape
    r = c_all.shape[0]
    return _pcall(
        _adaln_kernel, name="adaln", grid=(depth, n // tn),
        in_specs=[
            pl.BlockSpec((r, d), lambda l, j: (0, 0)),
            pl.BlockSpec((None, d, tn), lambda l, j: (l, 0, j)),
            pl.BlockSpec((None, 1, tn), lambda l, j: (l, 0, j)),
        ],
        out_specs=pl.BlockSpec((None, r, tn), lambda l, j: (l, 0, j)),
        out_shape=jax.ShapeDtypeStruct((depth, r, n), F32),
        args=(c_all, w_ada, b_ada.reshape(depth, 1, n)),
        temp_bytes=d * tn * 2 + r * (d + tn) * 8,
    )


def _qkv_kernel(x_ref, sh_ref, sc_ref, gpre_ref, wqkv_ref, gq_ref, gkv_ref, wn_ref, wp_ref, wuk_ref,
                cos_ref, sin_ref,
                qa_ref, qp_ref, ckv_ref, kpe_ref, ckvb_ref, kpeb_ref, *, heads, r_q, r_kv, nope, pe, q_scale):
    x = x_ref[...]
    h = _modulate(x, gpre_ref[...], sh_ref[...], sc_ref[...])
    lat = _dot(h.astype(BF16), wqkv_ref[...])
    cos = cos_ref[...]
    sin = sin_ref[...]

    qn = _rms(lat[:, :r_q], gq_ref[...]).astype(BF16)
    q_nope = _dot(qn, wn_ref[...])
    q_pe = _dot(qn, wp_ref[...])
    for hd in range(heads):
        qa = _dot(q_nope[:, hd * nope:(hd + 1) * nope].astype(BF16), wuk_ref[hd]) * q_scale
        qa_ref[hd] = qa.astype(qa_ref.dtype)
        qr = _rope_rotate(q_pe[:, hd * LANE:(hd + 1) * LANE], cos, sin) * q_scale
        qp_ref[hd] = qr[:, :qp_ref.shape[-1]].astype(qp_ref.dtype)

    ckv = _rms(lat[:, r_q:r_q + r_kv], gkv_ref[...])
    ckv_ref[...] = ckv
    ckvb_ref[...] = ckv.astype(BF16)
    kr = _rope_rotate(lat[:, r_q + r_kv:], cos, sin)
    kpe_ref[...] = kr[:, :pe]
    kpeb_ref[...] = kr.astype(BF16)


def _qkv(x2d, shift, scale, g_pre, w_qkv, g_q, g_kv, w_nope, w_pe, w_uk_t, cos_t, sin_t, *,
         layer, tm, mod_block, mod_index, rope_index, q_dtype, qp_width, q_scale):
    m, d = x2d.shape
    heads, nope, r_kv = w_uk_t.shape[1:]
    r_q = g_q.shape[-1]
    pe = 64
    nl = w_qkv.shape[2]
    lay2 = lambda i: (layer, 0, 0)
    kern = functools.partial(_qkv_kernel, heads=heads, r_q=r_q, r_kv=r_kv, nope=nope, pe=pe, q_scale=q_scale)
    const2 = lambda i: (0, 0)
    return _pcall(
        kern, name="qkv", grid=(m // tm,),
        in_specs=[
            pl.BlockSpec((tm, d), lambda i: (i, 0)),
            pl.BlockSpec(mod_block, mod_index),
            pl.BlockSpec(mod_block, mod_index),
            pl.BlockSpec((1, d), const2),
            pl.BlockSpec((None, d, nl), lay2),
            pl.BlockSpec((1, r_q), const2),
            pl.BlockSpec((1, r_kv), const2),
            pl.BlockSpec((None,) + w_nope.shape[1:], lay2),
            pl.BlockSpec((None,) + w_pe.shape[1:], lay2),
            pl.BlockSpec((None,) + w_uk_t.shape[1:], lambda i: (layer, 0, 0, 0)),
            pl.BlockSpec((tm, LANE), rope_index),
            pl.BlockSpec((tm, LANE), rope_index),
        ],
        out_specs=[
            pl.BlockSpec((heads, tm, r_kv), lambda i: (0, i, 0)),
            pl.BlockSpec((heads, tm, qp_width), lambda i: (0, i, 0)),
            pl.BlockSpec((tm, r_kv), lambda i: (i, 0)),
            pl.BlockSpec((tm, pe), lambda i: (i, 0)),
            pl.BlockSpec((tm, r_kv), lambda i: (i, 0)),
            pl.BlockSpec((tm, LANE), lambda i: (i, 0)),
        ],
        out_shape=[
            jax.ShapeDtypeStruct((heads, m, r_kv), q_dtype),
            jax.ShapeDtypeStruct((heads, m, qp_width), q_dtype),
            jax.ShapeDtypeStruct((m, r_kv), F32),
            jax.ShapeDtypeStruct((m, pe), F32),
            jax.ShapeDtypeStruct((m, r_kv), BF16),
            jax.ShapeDtypeStruct((m, LANE), BF16),
        ],
        args=(x2d, shift, scale, g_pre, w_qkv, g_q, g_kv, w_nope, w_pe, w_uk_t, cos_t, sin_t),
        temp_bytes=tm * (2 * d + nl + heads * (nope + LANE) + 2 * r_kv) * 4,
    )


def _gate_kernel(*refs, seq_rows, tiles_per_seq, has_prev, tc):
    if has_prev:
        (x_ref, sh_ref, sc_ref, gpre_ref, wb_ref, wc_ref, wx_ref, cw_ref, p0_ref, p1_ref,
         gc_ref, u_ref, carry_sc) = refs
    else:
        (x_ref, sh_ref, sc_ref, gpre_ref, wb_ref, wc_ref, wx_ref, cw_ref,
         gc_ref, u_ref, carry_sc) = refs
    i = pl.program_id(0)
    tm = x_ref.shape[0]
    cdim = cw_ref.shape[-1]

    if not has_prev:
        @pl.when(i % tiles_per_seq == 0)
        def _():
            carry_sc[...] = jnp.zeros(carry_sc.shape, F32)

    h = _modulate(x_ref[...], gpre_ref[...], sh_ref[...], sc_ref[...]).astype(BF16)
    row = lax.broadcasted_iota(jnp.int32, (tm, tc), 0)
    pos = row % seq_rows if has_prev else row
    for c0 in range(0, cdim, tc):
        cs = slice(c0, c0 + tc)
        b_gate = _dot(h, wb_ref[:, cs])
        u = _dot(h, wc_ref[:, cs]) * _dot(h, wx_ref[:, cs])
        if has_prev:
            p0 = p0_ref[:, cs]
            p1 = p1_ref[:, cs]
        else:
            p0 = carry_sc[SUBLANE - 2:SUBLANE - 1, cs]
            p1 = carry_sc[SUBLANE - 1:SUBLANE, cs]
        u1 = jnp.where(pos == 0, p1, pltpu.roll(u, 1, 0))
        u2 = jnp.where(pos == 0, p0, jnp.where(pos == 1, p1, pltpu.roll(u, 2, 0)))
        y = cw_ref[0:1, cs] * u2 + cw_ref[1:2, cs] * u1 + cw_ref[2:3, cs] * u
        gc_ref[:, cs] = (b_gate * y).astype(gc_ref.dtype)
        if has_prev:
            u_ref[:, cs] = u
        else:
            tail = u[tm - SUBLANE:, :]
            carry_sc[:, cs] = tail
            u_ref[:, cs] = tail


def _gate(x2d, shift, scale, g_pre, w_gate, conv_w, prev0, prev1, *, layer, tm, tc, mod_block, mod_index, seq_rows):
    m, d = x2d.shape
    cdim = conv_w.shape[-1]
    has_prev = prev0 is not None
    tiles_per_seq = max(seq_rows // tm, 1)
    kern = functools.partial(_gate_kernel, seq_rows=seq_rows, tiles_per_seq=tiles_per_seq,
                             has_prev=has_prev, tc=tc)
    in_specs = [
        pl.BlockSpec((tm, d), lambda i: (i, 0)),
        pl.BlockSpec(mod_block, mod_index),
        pl.BlockSpec(mod_block, mod_index),
        pl.BlockSpec((1, d), lambda i: (0, 0)),
        pl.BlockSpec((None, d, cdim), lambda i: (layer, 0, 0)),
        pl.BlockSpec((None, d, cdim), lambda i: (layer, 0, 1)),
        pl.BlockSpec((None, d, cdim), lambda i: (layer, 0, 2)),
        pl.BlockSpec((None, 3, cdim), lambda i: (layer, 0, 0)),
    ]
    args = [x2d, shift, scale, g_pre, w_gate, w_gate, w_gate, conv_w]
    if has_prev:
        in_specs += [pl.BlockSpec((tm, cdim), lambda i: (i, 0))] * 2
        args += [prev0, prev1]
        u_spec = pl.BlockSpec((tm, cdim), lambda i: (i, 0))
        u_shape = jax.ShapeDtypeStruct((m, cdim), F32)
    else:
        u_spec = pl.BlockSpec((None, SUBLANE, cdim), lambda i: (i, 0, 0))
        u_shape = jax.ShapeDtypeStruct((m // tm, SUBLANE, cdim), F32)
    return _pcall(
        kern, name="gate_conv", grid=(m // tm,),
        in_specs=in_specs,
        out_specs=[pl.BlockSpec((tm, cdim), lambda i: (i, 0)), u_spec],
        out_shape=[jax.ShapeDtypeStruct((m, cdim), BF16), u_shape],
        args=args,
        scratch=[((SUBLANE, cdim), F32)],
        temp_bytes=tm * d * 10 + tm * tc * 4 * 12,
    )


def _flash_kernel(qa_ref, qp_ref, kc_ref, kp_ref, wuv_ref, o_ref, m_sc, l_sc, acc_sc, *, tq, tk, heads, groups):
    qi = pl.program_id(1)
    hpg = heads // groups
    gr = hpg * tq
    nr = acc_sc.shape[-1] // LANE
    m_sc[...] = jnp.full(m_sc.shape, -jnp.inf, F32)
    l_sc[...] = jnp.zeros(l_sc.shape, F32)
    acc_sc[...] = jnp.zeros(acc_sc.shape, F32)

    def tile(j, masked):
        k0 = pl.multiple_of(j * tk, tk)
        kc = kc_ref[pl.ds(k0, tk), :]
        kp = kp_ref[pl.ds(k0, tk), :]
        for g in range(groups):
            rs = slice(g * gr, (g + 1) * gr)
            qa = qa_ref[g * hpg:(g + 1) * hpg].reshape(gr, qa_ref.shape[-1])
            qp = qp_ref[g * hpg:(g + 1) * hpg].reshape(gr, qp_ref.shape[-1])
            s = _dot_nt(qa, kc) + _dot_nt(qp, kp)
            if masked:
                qpos = lax.broadcasted_iota(jnp.int32, s.shape, 0) % tq
                kpos = lax.broadcasted_iota(jnp.int32, s.shape, 1)
                s = jnp.where(kpos - qpos <= qi * tq - j * tk, s, -jnp.inf)
            m_prev = m_sc[rs, :]
            m_new = jnp.maximum(m_prev, jnp.max(s, axis=-1, keepdims=True))
            alpha = jnp.exp2(m_prev - m_new)
            p = jnp.exp2(s - _lane_tile(m_new, tk // LANE))
            l_sc[rs, :] = alpha * l_sc[rs, :] + jnp.sum(p, axis=-1, keepdims=True)
            acc_sc[rs, :] = _lane_tile(alpha, nr) * acc_sc[rs, :] + _dot(p.astype(BF16), kc)
            m_sc[rs, :] = m_new

    n_full = (qi * tq) // tk

    def body(j, carry):
        tile(j, False)
        return carry

    lax.fori_loop(0, n_full, body, 0)
    tile(n_full, True)

    inv = _lane_tile(1.0 / l_sc[...], nr)
    dv = wuv_ref.shape[-1]
    for hd in range(heads):
        lat = (acc_sc[hd * tq:(hd + 1) * tq, :] * inv[hd * tq:(hd + 1) * tq, :]).astype(BF16)
        o_ref[:, hd * dv:(hd + 1) * dv] = _dot(lat, wuv_ref[hd]).astype(o_ref.dtype)


def _flash(qa, qp, ckv_b, kpe_b, w_uv_t, *, layer, batch, seq, tq, tk, groups):
    heads, m, r = qa.shape
    dv = w_uv_t.shape[-1]
    nq = seq // tq
    rows = heads * tq
    assert tk % tq == 0 and seq % tk == 0 and heads % groups == 0
    kern = functools.partial(_flash_kernel, tq=tq, tk=tk, heads=heads, groups=groups)
    return _pcall(
        kern, name="flash_prompt", grid=(batch, nq),
        in_specs=[
            pl.BlockSpec((heads, tq, r), lambda b, q: (0, b * nq + q, 0)),
            pl.BlockSpec((heads, tq, LANE), lambda b, q: (0, b * nq + q, 0)),
            pl.BlockSpec((seq, r), lambda b, q: (b, 0)),
            pl.BlockSpec((seq, LANE), lambda b, q: (b, 0)),
            pl.BlockSpec((None,) + w_uv_t.shape[1:], lambda b, q: (layer, 0, 0, 0)),
        ],
        out_specs=pl.BlockSpec((tq, heads * dv), lambda b, q: (b * nq + q, 0)),
        out_shape=jax.ShapeDtypeStruct((m, heads * dv), BF16),
        args=(qa, qp, ckv_b, kpe_b, w_uv_t),
        scratch=[((rows, LANE), F32), ((rows, LANE), F32), ((rows, r), F32)],
        temp_bytes=rows * tk * 4 * 4 + rows * r * 4,
    )


def _paged_kernel(pt_ref, qa_ref, qp_ref, cn_ref, kn_ref, ckv_hbm, kpe_hbm, o_ref,
                  m_sc, l_sc, acc_sc, kbuf, pbuf, sems, *, layer, pages, heads, t_new, nbuf, steps_per_seq):
    s = pl.program_id(0)
    total = pl.num_programs(0)
    g = s % steps_per_seq
    depth = nbuf - 1
    rows = heads * t_new
    nr = kbuf.shape[3] // LANE

    def copies(step, slot, p):
        pg = pt_ref[step * pages + p]
        return (pltpu.make_async_copy(ckv_hbm.at[layer, pg], kbuf.at[slot, p], sems.at[0, slot]),
                pltpu.make_async_copy(kpe_hbm.at[layer, pg], pbuf.at[slot, p], sems.at[1, slot]))

    @pl.when(s == 0)
    def _():
        for d in range(depth):
            for p in range(pages):
                for c in copies(d, d, p):
                    c.start()

    @pl.when(g == 0)
    def _():
        m_sc[...] = jnp.full(m_sc.shape, -jnp.inf, F32)
        l_sc[...] = jnp.zeros(l_sc.shape, F32)
        acc_sc[...] = jnp.zeros(acc_sc.shape, F32)

    slot = s % nbuf
    for p in range(pages):
        for c in copies(s, slot, p):
            c.wait()

    qa = qa_ref[...].reshape(rows, qa_ref.shape[-1]).astype(BF16)
    qp = qp_ref[...].reshape(rows, qp_ref.shape[-1]).astype(BF16)

    def merge(m_loc, l_loc, pv):
        m_prev = m_sc[...]
        m_new = jnp.maximum(m_prev, m_loc)
        alpha = jnp.exp2(m_prev - m_new)
        w = jnp.exp2(m_loc - m_new)
        l_sc[...] = alpha * l_sc[...] + w * l_loc
        acc_sc[...] = _lane_tile(alpha, nr) * acc_sc[...] + _lane_tile(w, nr) * pv
        m_sc[...] = m_new

    nxt = s + depth
    nxt_slot = nxt % nbuf
    nxt = jnp.where(nxt >= total, nxt - total, nxt)

    def pair_k(i):
        return jnp.concatenate([kbuf[slot, i].astype(BF16), kbuf[slot, i + 1].astype(BF16)], axis=0)

    def pair_kp(i):
        return jnp.concatenate([pbuf[slot, i].astype(BF16), pbuf[slot, i + 1].astype(BF16)], axis=1)

    s_pairs = []
    for i in range(0, pages, 2):
        s_pairs.append(_dot_nt(qa, pair_k(i)) + _dot(qp, pair_kp(i)))
        for j in (i, i + 1):
            for c in copies(nxt, nxt_slot, j):
                c.start()
    m_loc = s_pairs[0].max(axis=-1, keepdims=True)
    for sp in s_pairs[1:]:
        m_loc = jnp.maximum(m_loc, sp.max(axis=-1, keepdims=True))
    l_loc = None
    pv = None
    for n, i in enumerate(range(0, pages, 2)):
        pr = jnp.exp2(s_pairs[n] - m_loc)
        ls = jnp.sum(pr, axis=-1, keepdims=True)
        d = _dot(pr.astype(BF16), pair_k(i))
        l_loc = ls if l_loc is None else l_loc + ls
        pv = d if pv is None else pv + d
    merge(m_loc, l_loc, pv)

    @pl.when(g == steps_per_seq - 1)
    def _():
        kc = cn_ref[...].astype(BF16)
        kp = kn_ref[...].astype(BF16)
        sc = _dot_nt(qa, kc) + _dot_nt(qp, kp)
        tq = lax.broadcasted_iota(jnp.int32, sc.shape, 0) % t_new
        tk = lax.broadcasted_iota(jnp.int32, sc.shape, 1)
        sc = jnp.where(tk <= tq, sc, -jnp.inf)
        m_new = jnp.max(sc, axis=-1, keepdims=True)
        pr = jnp.exp2(sc - m_new)
        merge(m_new, jnp.sum(pr, axis=-1, keepdims=True), _dot(pr.astype(BF16), kc))
        o_ref[...] = (acc_sc[...] / _lane_tile(l_sc[...], nr)).reshape(o_ref.shape)

    @pl.when(s == total - 1)
    def _():
        for d in range(depth):
            for p in range(pages):
                for c in copies(d, (total + d) % nbuf, p):
                    c.wait()


def _paged(qa, qp, cache_ckv, cache_kpe_t, layer, page_table, ckv_new, kpe_new, *, pages, nbuf):
    heads, m, r = qa.shape
    pe = qp.shape[-1]
    nb, n_pages = page_table.shape
    t_new = m // nb
    page = cache_ckv.shape[2]
    rows = heads * t_new
    sps = n_pages // pages
    assert n_pages % pages == 0 and pages % 2 == 0 and nb * sps >= nbuf
    kern = functools.partial(_paged_kernel, layer=layer, pages=pages, heads=heads, t_new=t_new,
                             nbuf=nbuf, steps_per_seq=sps)
    return _pcall(
        kern, name="paged_sample", grid=(nb * sps,),
        in_specs=[pl.BlockSpec((heads, t_new, r), lambda s, pt: (0, s // sps, 0)),
                  pl.BlockSpec((heads, t_new, pe), lambda s, pt: (0, s // sps, 0)),
                  pl.BlockSpec((t_new, r), lambda s, pt: (s // sps, 0)),
                  pl.BlockSpec((t_new, pe), lambda s, pt: (s // sps, 0)),
                  pl.BlockSpec(memory_space=pl.ANY),
                  pl.BlockSpec(memory_space=pl.ANY)],
        out_specs=pl.BlockSpec((heads, t_new, r), lambda s, pt: (0, s // sps, 0)),
        out_shape=jax.ShapeDtypeStruct((heads, m, r), F32),
        args=(qa, qp, ckv_new, kpe_new, cache_ckv, cache_kpe_t),
        scratch=[((rows, LANE), F32), ((rows, LANE), F32), ((rows, r), F32),
                 ((nbuf, pages, page, r), F32), ((nbuf, pages, pe, page), F32)],
        semaphores=[pltpu.SemaphoreType.DMA((2, nbuf))],
        temp_bytes=rows * pages * page * 4 * 4 + pages * page * (r + pe) * 2,
        prefetch=(page_table.reshape(-1),),
    )


def _uv_kernel(lat_ref, w_ref, o_ref):
    o_ref[...] = _dot(lat_ref[...].astype(BF16), w_ref[...]).astype(o_ref.dtype)


def _uv_proj(lat, w_uv_t, *, layer):
    heads, m, r = lat.shape
    dv = w_uv_t.shape[-1]
    return _pcall(
        _uv_kernel, name="uv_proj", grid=(heads,),
        in_specs=[pl.BlockSpec((None, m, r), lambda h: (h, 0, 0)),
                  pl.BlockSpec((None, None, r, dv), lambda h: (layer, h, 0, 0))],
        out_specs=pl.BlockSpec((m, dv), lambda h: (0, h)),
        out_shape=jax.ShapeDtypeStruct((m, heads * dv), BF16),
        args=(lat, w_uv_t),
        temp_bytes=m * r * 2 + m * dv * 4,
    )


def _out_kernel(a_ref, gc_ref, wa_ref, wc_ref, x_ref, gate_ref, gpost_ref, o_ref):
    out = _dot(a_ref[...], wa_ref[...]) + _dot(gc_ref[...], wc_ref[...])
    o_ref[...] = _gated_residual(x_ref[...], gate_ref[...], _rms(out, gpost_ref[...]))


def _out_proj(attn, gconv, w_o, x2d, gate, g_post, *, layer, tm, mod_block, mod_index):
    m, d = x2d.shape
    ka, kc = attn.shape[1], gconv.shape[1]
    assert ka == kc and w_o.shape[1] == ka + kc
    return _pcall(
        _out_kernel, name="out_proj", grid=(m // tm,),
        in_specs=[
            pl.BlockSpec((tm, ka), lambda i: (i, 0)),
            pl.BlockSpec((tm, kc), lambda i: (i, 0)),
            pl.BlockSpec((None, ka, d), lambda i: (layer, 0, 0)),
            pl.BlockSpec((None, kc, d), lambda i: (layer, 1, 0)),
            pl.BlockSpec((tm, d), lambda i: (i, 0)),
            pl.BlockSpec(mod_block, mod_index),
            pl.BlockSpec((1, d), lambda i: (0, 0)),
        ],
        out_specs=pl.BlockSpec((tm, d), lambda i: (i, 0)),
        out_shape=jax.ShapeDtypeStruct((m, d), F32),
        args=(attn, gconv, w_o, w_o, x2d, gate, g_post),
        temp_bytes=tm * d * 4 * 3,
    )


def _mlp_kernel(x_ref, sh_ref, sc_ref, gate_ref, gpre_ref, gpost_ref, wu_ref, wd_ref, o_ref, h_sc, acc_sc):
    f = pl.program_id(1)

    @pl.when(f == 0)
    def _():
        h = _modulate(x_ref[...], gpre_ref[...], sh_ref[...], sc_ref[...])
        h_sc[...] = h.astype(BF16)
        acc_sc[...] = jnp.zeros(acc_sc.shape, F32)

    a = jnp.maximum(_dot(h_sc[...], wu_ref[...]), 0.0)
    acc_sc[...] += _dot((a * a).astype(BF16), wd_ref[...])

    @pl.when(f == pl.num_programs(1) - 1)
    def _():
        o_ref[...] = _gated_residual(x_ref[...], gate_ref[...], _rms(acc_sc[...], gpost_ref[...]))


def _mlp(x2d, shift, scale, gate, g_pre, g_post, w_up, w_down, *, layer, tm, tf, mod_block, mod_index):
    m, d = x2d.shape
    ff = w_up.shape[2]
    mi = lambda i, f: mod_index(i)
    return _pcall(
        _mlp_kernel, name="mlp", grid=(m // tm, ff // tf),
        in_specs=[
            pl.BlockSpec((tm, d), lambda i, f: (i, 0)),
            pl.BlockSpec(mod_block, mi),
            pl.BlockSpec(mod_block, mi),
            pl.BlockSpec(mod_block, mi),
            pl.BlockSpec((1, d), lambda i, f: (0, 0)),
            pl.BlockSpec((1, d), lambda i, f: (0, 0)),
            pl.BlockSpec((None, d, tf), lambda i, f: (layer, 0, f)),
            pl.BlockSpec((None, tf, d), lambda i, f: (layer, f, 0)),
        ],
        out_specs=pl.BlockSpec((tm, d), lambda i, f: (i, 0)),
        out_shape=jax.ShapeDtypeStruct((m, d), F32),
        args=(x2d, shift, scale, gate, g_pre, g_post, w_up, w_down),
        scratch=[((tm, d), BF16), ((tm, d), F32)],
        temp_bytes=tm * tf * 4 * 3 + tm * d * 4 * 2,
    )


def _rope_tables(pos):
    half = 32
    inv = ROPE_THETA ** (-jnp.arange(half, dtype=F32) / half)
    ang = pos.astype(F32)[:, None] * inv[None, :]
    cos, sin = jnp.cos(ang), jnp.sin(ang)
    zero = jnp.zeros((pos.shape[0], LANE - 2 * half), F32)
    return (jnp.concatenate([cos, cos, zero], axis=-1),
            jnp.concatenate([-sin, sin, zero], axis=-1))


def kernel(x_prompt, x_sample, c_prompt, c_sample, cache_ckv, cache_kpe, state_conv, page_table, w_ada, b_ada,
           g_pre_mix, g_post_mix, w_in, g_q_lat, g_kv_lat, w_uq, w_uk, w_uv, conv_w, w_o, g_pre_mlp, g_post_mlp,
           w_up, w_down):
    batch, seq, d = x_prompt.shape
    nb, t_new, _ = x_sample.shape
    depth = w_in.shape[0]
    r_q, heads, qk = w_uq.shape[1:]
    r_kv, _, nope = w_uk.shape[1:]
    dv = w_uv.shape[-1]
    pe = qk - nope
    cdim = conv_w.shape[-1]
    past = page_table.shape[1] * cache_ckv.shape[2]
    q_scale = math.log2(math.e) / math.sqrt(qk)
    mp_rows, ms_rows = batch * seq, nb * t_new

    tm_p, tm_s = 512, 512
    tps = seq // tm_p

    n_c = batch + nb
    pad = (-n_c) % SUBLANE
    c_all = jnp.concatenate([c_prompt, c_sample, jnp.zeros((pad, d), F32)], axis=0)
    mod = _adaln(c_all, w_ada, b_ada)

    cos_p, sin_p = _rope_tables(jnp.arange(seq, dtype=jnp.int32))
    cos_s, sin_s = _rope_tables(past + jnp.arange(t_new, dtype=jnp.int32))
    cos_s = jnp.tile(cos_s, (tm_s // t_new, 1))
    sin_s = jnp.tile(sin_s, (tm_s // t_new, 1))

    mod_index_p = lambda i: (i // tps, 0, 0)
    mod_index_s = lambda i: (i, 0)
    rope_index_p = lambda i: (i % tps, 0)
    rope_index_s = lambda i: (0, 0)
    kw_p = dict(tm=tm_p, mod_block=(1, 1, d), mod_index=mod_index_p)
    kw_s = dict(tm=tm_s, mod_block=(tm_s // t_new, d), mod_index=mod_index_s)
    kw_gs = dict(tm=tm_s // 2, mod_block=(tm_s // 2 // t_new, d), mod_index=mod_index_s)

    cache_kpe_t = jnp.swapaxes(cache_kpe, 2, 3)

    s3 = r_q + r_kv + pe
    w_qkv = jnp.concatenate([w_in[:, :, :s3], jnp.zeros((depth, d, LANE - pe), F32)], axis=2).astype(BF16)
    w_gate = w_in[:, :, s3:].astype(BF16)
    w_nope = w_uq[..., :nope].reshape(depth, r_q, heads * nope).astype(BF16)
    w_pe = jnp.concatenate([w_uq[..., nope:], jnp.zeros((depth, r_q, heads, LANE - pe), F32)],
                           axis=-1).reshape(depth, r_q, heads * LANE).astype(BF16)
    w_uk_t = jnp.transpose(w_uk, (0, 2, 3, 1)).astype(BF16)
    w_uv_t = jnp.transpose(w_uv, (0, 2, 1, 3)).astype(BF16)
    w_o_b = w_o.astype(BF16)
    w_up_b = w_up.astype(BF16)
    w_down_b = w_down.astype(BF16)

    x_p = x_prompt.reshape(mp_rows, d)
    x_s = x_sample.reshape(ms_rows, d)
    outs = [[] for _ in range(6)]
    for l in range(depth):
        g_pre = g_pre_mix[l].reshape(1, d)
        g_post = g_post_mix[l].reshape(1, d)
        g_q = g_q_lat[l].reshape(1, r_q)
        g_kv = g_kv_lat[l].reshape(1, r_kv)
        g_pre2 = g_pre_mlp[l].reshape(1, d)
        g_post2 = g_post_mlp[l].reshape(1, d)

        mods_p = [mod[l, :batch, k * d:(k + 1) * d].reshape(batch, 1, d) for k in range(6)]
        mods_s = [mod[l, batch:batch + nb, k * d:(k + 1) * d] for k in range(6)]

        qa, qp, ckv_p, kpe_p, ckv_pb, kpe_pb = _qkv(
            x_p, mods_p[0], mods_p[1], g_pre, w_qkv, g_q, g_kv, w_nope, w_pe, w_uk_t, cos_p, sin_p,
            layer=l, rope_index=rope_index_p, q_dtype=BF16, qp_width=LANE, q_scale=q_scale, **kw_p)
        gconv_p, tail_p = _gate(x_p, mods_p[0], mods_p[1], g_pre, w_gate, conv_w, None, None,
                                layer=l, tc=512, seq_rows=seq, **kw_p)
        attn_p = _flash(qa, qp, ckv_pb, kpe_pb, w_uv_t, layer=l, batch=batch, seq=seq, tq=256, tk=512, groups=2)
        x_p = _out_proj(attn_p, gconv_p, w_o_b, x_p, mods_p[2], g_post, layer=l, **kw_p)
        x_p = _mlp(x_p, mods_p[3], mods_p[4], mods_p[5], g_pre2, g_post2, w_up_b, w_down_b,
                   layer=l, tf=1024, **kw_p)

        qa, qp, ckv_s, kpe_s, _, _ = _qkv(
            x_s, mods_s[0], mods_s[1], g_pre, w_qkv, g_q, g_kv, w_nope, w_pe, w_uk_t, cos_s, sin_s,
            layer=l, rope_index=rope_index_s, q_dtype=F32, qp_width=pe, q_scale=q_scale, **kw_s)
        prev0 = jnp.repeat(state_conv[l][:, 0, :], t_new, axis=0)
        prev1 = jnp.repeat(state_conv[l][:, 1, :], t_new, axis=0)
        gconv_s, u_s = _gate(x_s, mods_s[0], mods_s[1], g_pre, w_gate, conv_w, prev0, prev1,
                             layer=l, tc=512, seq_rows=t_new, **kw_gs)
        lat_s = _paged(qa, qp, cache_ckv, cache_kpe_t, l, page_table, ckv_s, kpe_s, pages=32, nbuf=3)
        attn_s = _uv_proj(lat_s, w_uv_t, layer=l)
        x_s = _out_proj(attn_s, gconv_s, w_o_b, x_s, mods_s[2], g_post, layer=l, **kw_s)
        x_s = _mlp(x_s, mods_s[3], mods_s[4], mods_s[5], g_pre2, g_post2, w_up_b, w_down_b,
                   layer=l, tf=1024, **kw_s)

        outs[0].append(ckv_p.reshape(batch, seq, r_kv))
        outs[1].append(kpe_p.reshape(batch, seq, pe))
        outs[2].append(tail_p[tps - 1::tps, SUBLANE - 2:, :])
        outs[3].append(ckv_s.reshape(nb, t_new, r_kv))
        outs[4].append(kpe_s.reshape(nb, t_new, pe))
        outs[5].append(u_s.reshape(nb, t_new, cdim)[:, t_new - 2:, :])

    return (x_p.reshape(batch, seq, d), x_s.reshape(nb, t_new, d),
            jnp.stack(outs[0]), jnp.stack(outs[1]), jnp.stack(outs[2]),
            jnp.stack(outs[3]), jnp.stack(outs[4]), jnp.stack(outs[5]))
```

```python
import functools
import math

import jax
import jax.numpy as jnp
from jax import lax
from jax.experimental import pallas as pl
from jax.experimental.pallas import tpu as pltpu

NORM_EPS = 1e-6
ROPE_THETA = 10000.0
LANE = 128
SUBLANE = 8
VMEM_CAP_BYTES = 60 * 1024 * 1024

F32 = jnp.float32
BF16 = jnp.bfloat16


def _padded_bytes(shape, dtype):
    dims = [d or 1 for d in shape]
    dims[-1] = -(-dims[-1] // LANE) * LANE
    if len(dims) > 1:
        dims[-2] = -(-dims[-2] // SUBLANE) * SUBLANE
    n = jnp.dtype(dtype).itemsize
    for d in dims:
        n *= d
    return n


def _pcall(body, *, name, grid, in_specs, out_specs, out_shape, args, scratch=(), semaphores=(), temp_bytes=0,
           prefetch=()):
    outs = out_shape if isinstance(out_shape, (list, tuple)) else [out_shape]
    ospecs = out_specs if isinstance(out_specs, (list, tuple)) else [out_specs]
    block_bytes = sum(_padded_bytes(sp.block_shape, a.dtype) for sp, a in zip(in_specs, args)
                      if sp.block_shape is not None)
    block_bytes += sum(_padded_bytes(sp.block_shape, o.dtype) for sp, o in zip(ospecs, outs))
    scratch_bytes = sum(_padded_bytes(sh, dt) for sh, dt in scratch)
    vmem = 2 * block_bytes + scratch_bytes + temp_bytes + (4 << 20)
    grid_spec = pltpu.PrefetchScalarGridSpec(
        num_scalar_prefetch=len(prefetch), grid=grid, in_specs=list(in_specs), out_specs=out_specs,
        scratch_shapes=[pltpu.VMEM(sh, dt) for sh, dt in scratch] + list(semaphores))
    return pl.pallas_call(
        body, grid_spec=grid_spec, out_shape=out_shape,
        compiler_params=pltpu.CompilerParams(dimension_semantics=("arbitrary",) * len(grid),
                                             vmem_limit_bytes=int(min(vmem, VMEM_CAP_BYTES))),
        name=name,
    )(*prefetch, *args)


def _dot(a, b):
    return jnp.dot(a, b, preferred_element_type=F32)


def _dot_nt(a, b):
    return lax.dot_general(a, b, (((1,), (1,)), ((), ())), preferred_element_type=F32)


def _rms(x, g):
    ms = jnp.mean(x * x, axis=-1, keepdims=True)
    return x * lax.rsqrt(ms + NORM_EPS) * g


def _rope_rotate(x, cos, sin_signed):
    lane = lax.broadcasted_iota(jnp.int32, x.shape, 1)
    rot = jnp.where(lane < 32, pltpu.roll(x, LANE - 32, 1), pltpu.roll(x, 32, 1))
    return x * cos + rot * sin_signed


def _per_seq(x, mod, fn):
    if mod.ndim == 3:
        return fn(x, mod[0])
    n = mod.shape[0]
    rows, d = x.shape
    return fn(x.reshape(n, rows // n, d), mod[:, None, :]).reshape(rows, d)


def _modulate(x, g, shift, scale):
    y = _per_seq(_rms(x, g), scale, lambda a, sc: a * (1.0 + sc))
    return _per_seq(y, shift, lambda a, sh: a + sh)


def _gated_residual(x, gate, y):
    return x + _per_seq(y, gate, lambda a, gt: a * gt)


def _lane_tile(x, n):
    return jnp.concatenate([x] * n, axis=1) if n > 1 else x


def _adaln_kernel(c_ref, w_ref, b_ref, o_ref):
    c = c_ref[...]
    s = c * (1.0 / (1.0 + jnp.exp(-c)))
    o_ref[...] = _dot(s.astype(BF16), w_ref[...].astype(BF16)) + b_ref[...]


def _adaln(c_all, w_ada, b_ada, tn=1024):
    depth, d, n = w_ada.shape
    r = c_all.shape[0]
    return _pcall(
        _adaln_kernel, name="adaln", grid=(depth, n // tn),
        in_specs=[
            pl.BlockSpec((r, d), lambda l, j: (0, 0)),
            pl.BlockSpec((None, d, tn), lambda l, j: (l, 0, j)),
            pl.BlockSpec((None, 1, tn), lambda l, j: (l, 0, j)),
        ],
        out_specs=pl.BlockSpec((None, r, tn), lambda l, j: (l, 0, j)),
        out_shape=jax.ShapeDtypeStruct((depth, r, n), F32),
        args=(c_all, w_ada, b_ada.reshape(depth, 1, n)),
        temp_bytes=d * tn * 2 + r * (d + tn) * 8,
    )


def _qkv_kernel(x_ref, sh_ref, sc_ref, gpre_ref, wqkv_ref, gq_ref, gkv_ref, wn_ref, wp_ref, wuk_ref,
                cos_ref, sin_ref,
                qa_ref, qp_ref, ckv_ref, kpe_ref, ckvb_ref, kpeb_ref, *, heads, r_q, r_kv, nope, pe, q_scale):
    x = x_ref[...]
    h = _modulate(x, gpre_ref[...], sh_ref[...], sc_ref[...])
    lat = _dot(h.astype(BF16), wqkv_ref[...])
    cos = cos_ref[...]
    sin = sin_ref[...]

    qn = _rms(lat[:, :r_q], gq_ref[...]).astype(BF16)
    q_nope = _dot(qn, wn_ref[...])
    q_pe = _dot(qn, wp_ref[...])
    for hd in range(heads):
        qa = _dot(q_nope[:, hd * nope:(hd + 1) * nope].astype(BF16), wuk_ref[hd]) * q_scale
        qa_ref[hd] = qa.astype(qa_ref.dtype)
        qr = _rope_rotate(q_pe[:, hd * LANE:(hd + 1) * LANE], cos, sin) * q_scale
        qp_ref[hd] = qr[:, :qp_ref.shape[-1]].astype(qp_ref.dtype)

    ckv = _rms(lat[:, r_q:r_q + r_kv], gkv_ref[...])
    ckv_ref[...] = ckv
    ckvb_ref[...] = ckv.astype(BF16)
    kr = _rope_rotate(lat[:, r_q + r_kv:], cos, sin)
    kpe_ref[...] = kr[:, :pe]
    kpeb_ref[...] = kr.astype(BF16)


def _qkv(x2d, shift, scale, g_pre, w_qkv, g_q, g_kv, w_nope, w_pe, w_uk_t, cos_t, sin_t, *,
         layer, tm, mod_block, mod_index, rope_index, q_dtype, qp_width, q_scale):
    m, d = x2d.shape
    heads, nope, r_kv = w_uk_t.shape[1:]
    r_q = g_q.shape[-1]
    pe = 64
    nl = w_qkv.shape[2]
    lay2 = lambda i: (layer, 0, 0)
    kern = functools.partial(_qkv_kernel, heads=heads, r_q=r_q, r_kv=r_kv, nope=nope, pe=pe, q_scale=q_scale)
    const2 = lambda i: (0, 0)
    return _pcall(
        kern, name="qkv", grid=(m // tm,),
        in_specs=[
            pl.BlockSpec((tm, d), lambda i: (i, 0)),
            pl.BlockSpec(mod_block, mod_index),
            pl.BlockSpec(mod_block, mod_index),
            pl.BlockSpec((1, d), const2),
            pl.BlockSpec((None, d, nl), lay2),
            pl.BlockSpec((1, r_q), const2),
            pl.BlockSpec((1, r_kv), const2),
            pl.BlockSpec((None,) + w_nope.shape[1:], lay2),
            pl.BlockSpec((None,) + w_pe.shape[1:], lay2),
            pl.BlockSpec((None,) + w_uk_t.shape[1:], lambda i: (layer, 0, 0, 0)),
            pl.BlockSpec((tm, LANE), rope_index),
            pl.BlockSpec((tm, LANE), rope_index),
        ],
        out_specs=[
            pl.BlockSpec((heads, tm, r_kv), lambda i: (0, i, 0)),
            pl.BlockSpec((heads, tm, qp_width), lambda i: (0, i, 0)),
            pl.BlockSpec((tm, r_kv), lambda i: (i, 0)),
            pl.BlockSpec((tm, pe), lambda i: (i, 0)),
            pl.BlockSpec((tm, r_kv), lambda i: (i, 0)),
            pl.BlockSpec((tm, LANE), lambda i: (i, 0)),
        ],
        out_shape=[
            jax.ShapeDtypeStruct((heads, m, r_kv), q_dtype),
            jax.ShapeDtypeStruct((heads, m, qp_width), q_dtype),
            jax.ShapeDtypeStruct((m, r_kv), F32),
            jax.ShapeDtypeStruct((m, pe), F32),
            jax.ShapeDtypeStruct((m, r_kv), BF16),
            jax.ShapeDtypeStruct((m, LANE), BF16),
        ],
        args=(x2d, shift, scale, g_pre, w_qkv, g_q, g_kv, w_nope, w_pe, w_uk_t, cos_t, sin_t),
        temp_bytes=tm * (2 * d + nl + heads * (nope + LANE) + 2 * r_kv) * 4,
    )


def _gate_kernel(*refs, seq_rows, tiles_per_seq, has_prev, tc):
    if has_prev:
        (x_ref, sh_ref, sc_ref, gpre_ref, wb_ref, wc_ref, wx_ref, cw_ref, p0_ref, p1_ref,
         gc_ref, u_ref, carry_sc) = refs
    else:
        (x_ref, sh_ref, sc_ref, gpre_ref, wb_ref, wc_ref, wx_ref, cw_ref,
         gc_ref, u_ref, carry_sc) = refs
    i = pl.program_id(0)
    tm = x_ref.shape[0]
    cdim = cw_ref.shape[-1]

    if not has_prev:
        @pl.when(i % tiles_per_seq == 0)
        def _():
            carry_sc[...] = jnp.zeros(carry_sc.shape, F32)

    h = _modulate(x_ref[...], gpre_ref[...], sh_ref[...], sc_ref[...]).astype(BF16)
    row = lax.broadcasted_iota(jnp.int32, (tm, tc), 0)
    pos = row % seq_rows if has_prev else row
    for c0 in range(0, cdim, tc):
        cs = slice(c0, c0 + tc)
        b_gate = _dot(h, wb_ref[:, cs])
        u = _dot(h, wc_ref[:, cs]) * _dot(h, wx_ref[:, cs])
        if has_prev:
            p0 = p0_ref[:, cs]
            p1 = p1_ref[:, cs]
        else:
            p0 = carry_sc[SUBLANE - 2:SUBLANE - 1, cs]
            p1 = carry_sc[SUBLANE - 1:SUBLANE, cs]
        u1 = jnp.where(pos == 0, p1, pltpu.roll(u, 1, 0))
        u2 = jnp.where(pos == 0, p0, jnp.where(pos == 1, p1, pltpu.roll(u, 2, 0)))
        y = cw_ref[0:1, cs] * u2 + cw_ref[1:2, cs] * u1 + cw_ref[2:3, cs] * u
        gc_ref[:, cs] = (b_gate * y).astype(gc_ref.dtype)
        if has_prev:
            u_ref[:, cs] = u
        else:
            tail = u[tm - SUBLANE:, :]
            carry_sc[:, cs] = tail
            u_ref[:, cs] = tail


def _gate(x2d, shift, scale, g_pre, w_gate, conv_w, prev0, prev1, *, layer, tm, tc, mod_block, mod_index, seq_rows):
    m, d = x2d.shape
    cdim = conv_w.shape[-1]
    has_prev = prev0 is not None
    tiles_per_seq = max(seq_rows // tm, 1)
    kern = functools.partial(_gate_kernel, seq_rows=seq_rows, tiles_per_seq=tiles_per_seq,
                             has_prev=has_prev, tc=tc)
    in_specs = [
        pl.BlockSpec((tm, d), lambda i: (i, 0)),
        pl.BlockSpec(mod_block, mod_index),
        pl.BlockSpec(mod_block, mod_index),
        pl.BlockSpec((1, d), lambda i: (0, 0)),
        pl.BlockSpec((None, d, cdim), lambda i: (layer, 0, 0)),
        pl.BlockSpec((None, d, cdim), lambda i: (layer, 0, 1)),
        pl.BlockSpec((None, d, cdim), lambda i: (layer, 0, 2)),
        pl.BlockSpec((None, 3, cdim), lambda i: (layer, 0, 0)),
    ]
    args = [x2d, shift, scale, g_pre, w_gate, w_gate, w_gate, conv_w]
    if has_prev:
        in_specs += [pl.BlockSpec((tm, cdim), lambda i: (i, 0))] * 2
        args += [prev0, prev1]
        u_spec = pl.BlockSpec((tm, cdim), lambda i: (i, 0))
        u_shape = jax.ShapeDtypeStruct((m, cdim), F32)
    else:
        u_spec = pl.BlockSpec((None, SUBLANE, cdim), lambda i: (i, 0, 0))
        u_shape = jax.ShapeDtypeStruct((m // tm, SUBLANE, cdim), F32)
    return _pcall(
        kern, name="gate_conv", grid=(m // tm,),
        in_specs=in_specs,
        out_specs=[pl.BlockSpec((tm, cdim), lambda i: (i, 0)), u_spec],
        out_shape=[jax.ShapeDtypeStruct((m, cdim), BF16), u_shape],
        args=args,
        scratch=[((SUBLANE, cdim), F32)],
        temp_bytes=tm * d * 10 + tm * tc * 4 * 12,
    )


def _flash_kernel(qa_ref, qp_ref, kc_ref, kp_ref, wuv_ref, o_ref, m_sc, l_sc, acc_sc, *, tq, tk, heads, groups):
    qi = pl.program_id(1)
    hpg = heads // groups
    gr = hpg * tq
    nr = acc_sc.shape[-1] // LANE
    m_sc[...] = jnp.full(m_sc.shape, -jnp.inf, F32)
    l_sc[...] = jnp.zeros(l_sc.shape, F32)
    acc_sc[...] = jnp.zeros(acc_sc.shape, F32)

    def tile(j, masked):
        k0 = pl.multiple_of(j * tk, tk)
        kc = kc_ref[pl.ds(k0, tk), :]
        kp = kp_ref[pl.ds(k0, tk), :]
        for g in range(groups):
            rs = slice(g * gr, (g + 1) * gr)
            qa = qa_ref[g * hpg:(g + 1) * hpg].reshape(gr, qa_ref.shape[-1])
            qp = qp_ref[g * hpg:(g + 1) * hpg].reshape(gr, qp_ref.shape[-1])
            s = _dot_nt(qa, kc) + _dot_nt(qp, kp)
            if masked:
                qpos = lax.broadcasted_iota(jnp.int32, s.shape, 0) % tq
                kpos = lax.broadcasted_iota(jnp.int32, s.shape, 1)
                s = jnp.where(kpos - qpos <= qi * tq - j * tk, s, -jnp.inf)
            m_prev = m_sc[rs, :]
            m_new = jnp.maximum(m_prev, jnp.max(s, axis=-1, keepdims=True))
            alpha = jnp.exp2(m_prev - m_new)
            p = jnp.exp2(s - _lane_tile(m_new, tk // LANE))
            l_sc[rs, :] = alpha * l_sc[rs, :] + jnp.sum(p, axis=-1, keepdims=True)
            acc_sc[rs, :] = _lane_tile(alpha, nr) * acc_sc[rs, :] + _dot(p.astype(BF16), kc)
            m_sc[rs, :] = m_new

    n_full = (qi * tq) // tk

    def body(j, carry):
        tile(j, False)
        return carry

    lax.fori_loop(0, n_full, body, 0)
    tile(n_full, True)

    inv = _lane_tile(1.0 / l_sc[...], nr)
    dv = wuv_ref.shape[-1]
    for hd in range(heads):
        lat = (acc_sc[hd * tq:(hd + 1) * tq, :] * inv[hd * tq:(hd + 1) * tq, :]).astype(BF16)
        o_ref[:, hd * dv:(hd + 1) * dv] = _dot(lat, wuv_ref[hd]).astype(o_ref.dtype)


def _flash(qa, qp, ckv_b, kpe_b, w_uv_t, *, layer, batch, seq, tq, tk, groups):
    heads, m, r = qa.shape
    dv = w_uv_t.shape[-1]
    nq = seq // tq
    rows = heads * tq
    assert tk % tq == 0 and seq % tk == 0 and heads % groups == 0
    kern = functools.partial(_flash_kernel, tq=tq, tk=tk, heads=heads, groups=groups)
    return _pcall(
        kern, name="flash_prompt", grid=(batch, nq),
        in_specs=[
            pl.BlockSpec((heads, tq, r), lambda b, q: (0, b * nq + q, 0)),
            pl.BlockSpec((heads, tq, LANE), lambda b, q: (0, b * nq + q, 0)),
            pl.BlockSpec((seq, r), lambda b, q: (b, 0)),
            pl.BlockSpec((seq, LANE), lambda b, q: (b, 0)),
            pl.BlockSpec((None,) + w_uv_t.shape[1:], lambda b, q: (layer, 0, 0, 0)),
        ],
        out_specs=pl.BlockSpec((tq, heads * dv), lambda b, q: (b * nq + q, 0)),
        out_shape=jax.ShapeDtypeStruct((m, heads * dv), BF16),
        args=(qa, qp, ckv_b, kpe_b, w_uv_t),
        scratch=[((rows, LANE), F32), ((rows, LANE), F32), ((rows, r), F32)],
        temp_bytes=rows * tk * 4 * 4 + rows * r * 4,
    )


def _paged_kernel(pt_ref, qa_ref, qp_ref, cn_ref, kn_ref, ckv_hbm, kpe_hbm, o_ref,
                  m_sc, l_sc, acc_sc, kbuf, pbuf, sems, *, layer, pages, heads, t_new, nbuf, steps_per_seq):
    s = pl.program_id(0)
    total = pl.num_programs(0)
    g = s % steps_per_seq
    depth = nbuf - 1
    rows = heads * t_new
    nr = kbuf.shape[3] // LANE

    def copies(step, slot, p):
        pg = pt_ref[step * pages + p]
        return (pltpu.make_async_copy(ckv_hbm.at[layer, pg], kbuf.at[slot, p], sems.at[0, slot]),
                pltpu.make_async_copy(kpe_hbm.at[layer, pg], pbuf.at[slot, p], sems.at[1, slot]))

    @pl.when(s == 0)
    def _():
        for d in range(depth):
            for p in range(pages):
                for c in copies(d, d, p):
                    c.start()

    @pl.when(g == 0)
    def _():
        m_sc[...] = jnp.full(m_sc.shape, -jnp.inf, F32)
        l_sc[...] = jnp.zeros(l_sc.shape, F32)
        acc_sc[...] = jnp.zeros(acc_sc.shape, F32)

    slot = s % nbuf
    for p in range(pages):
        for c in copies(s, slot, p):
            c.wait()

    qa = qa_ref[...].reshape(rows, qa_ref.shape[-1]).astype(BF16)
    qp = qp_ref[...].reshape(rows, qp_ref.shape[-1]).astype(BF16)

    def merge(m_loc, l_loc, pv):
        m_prev = m_sc[...]
        m_new = jnp.maximum(m_prev, m_loc)
        alpha = jnp.exp2(m_prev - m_new)
        w = jnp.exp2(m_loc - m_new)
        l_sc[...] = alpha * l_sc[...] + w * l_loc
        acc_sc[...] = _lane_tile(alpha, nr) * acc_sc[...] + _lane_tile(w, nr) * pv
        m_sc[...] = m_new

    nxt = s + depth
    nxt_slot = nxt % nbuf
    nxt = jnp.where(nxt >= total, nxt - total, nxt)

    def pair_k(i):
        return jnp.concatenate([kbuf[slot, i].astype(BF16), kbuf[slot, i + 1].astype(BF16)], axis=0)

    def pair_kp(i):
        return jnp.concatenate([pbuf[slot, i].astype(BF16), pbuf[slot, i + 1].astype(BF16)], axis=1)

    s_pairs = []
    for i in range(0, pages, 2):
        s_pairs.append(_dot_nt(qa, pair_k(i)) + _dot(qp, pair_kp(i)))
        for j in (i, i + 1):
            for c in copies(nxt, nxt_slot, j):
                c.start()
    m_loc = s_pairs[0].max(axis=-1, keepdims=True)
    for sp in s_pairs[1:]:
        m_loc = jnp.maximum(m_loc, sp.max(axis=-1, keepdims=True))
    l_loc = None
    pv = None
    for n, i in enumerate(range(0, pages, 2)):
        pr = jnp.exp2(s_pairs[n] - m_loc)
        ls = jnp.sum(pr, axis=-1, keepdims=True)
        d = _dot(pr.astype(BF16), pair_k(i))
        l_loc = ls if l_loc is None else l_loc + ls
        pv = d if pv is None else pv + d
    merge(m_loc, l_loc, pv)

    @pl.when(g == steps_per_seq - 1)
    def _():
        kc = cn_ref[...].astype(BF16)
        kp = kn_ref[...].astype(BF16)
        sc = _dot_nt(qa, kc) + _dot_nt(qp, kp)
        tq = lax.broadcasted_iota(jnp.int32, sc.shape, 0) % t_new
        tk = lax.broadcasted_iota(jnp.int32, sc.shape, 1)
        sc = jnp.where(tk <= tq, sc, -jnp.inf)
        m_new = jnp.max(sc, axis=-1, keepdims=True)
        pr = jnp.exp2(sc - m_new)
        merge(m_new, jnp.sum(pr, axis=-1, keepdims=True), _dot(pr.astype(BF16), kc))
        o_ref[...] = (acc_sc[...] / _lane_tile(l_sc[...], nr)).reshape(o_ref.shape)

    @pl.when(s == total - 1)
    def _():
        for d in range(depth):
            for p in range(pages):
                for c in copies(d, (total + d) % nbuf, p):
                    c.wait()


def _paged(qa, qp, cache_ckv, cache_kpe_t, layer, page_table, ckv_new, kpe_new, *, pages, nbuf):
    heads, m, r = qa.shape
    pe = qp.shape[-1]
    nb, n_pages = page_table.shape
    t_new = m // nb
    page = cache_ckv.shape[2]
    rows = heads * t_new
    sps = n_pages // pages
    assert n_pages % pages == 0 and pages % 2 == 0 and nb * sps >= nbuf
    kern = functools.partial(_paged_kernel, layer=layer, pages=pages, heads=heads, t_new=t_new,
                             nbuf=nbuf, steps_per_seq=sps)
    return _pcall(
        kern, name="paged_sample", grid=(nb * sps,),
        in_specs=[pl.BlockSpec((heads, t_new, r), lambda s, pt: (0, s // sps, 0)),
                  pl.BlockSpec((heads, t_new, pe), lambda s, pt: (0, s // sps, 0)),
                  pl.BlockSpec((t_new, r), lambda s, pt: (s // sps, 0)),
                  pl.BlockSpec((t_new, pe), lambda s, pt: (s // sps, 0)),
                  pl.BlockSpec(memory_space=pl.ANY),
                  pl.BlockSpec(memory_space=pl.ANY)],
        out_specs=pl.BlockSpec((heads, t_new, r), lambda s, pt: (0, s // sps, 0)),
        out_shape=jax.ShapeDtypeStruct((heads, m, r), F32),
        args=(qa, qp, ckv_new, kpe_new, cache_ckv, cache_kpe_t),
        scratch=[((rows, LANE), F32), ((rows, LANE), F32), ((rows, r), F32),
                 ((nbuf, pages, page, r), F32), ((nbuf, pages, pe, page), F32)],
        semaphores=[pltpu.SemaphoreType.DMA((2, nbuf))],
        temp_bytes=rows * pages * page * 4 * 4 + pages * page * (r + pe) * 2,
        prefetch=(page_table.reshape(-1),),
    )


def _uv_kernel(lat_ref, w_ref, o_ref):
    o_ref[...] = _dot(lat_ref[...].astype(BF16), w_ref[...]).astype(o_ref.dtype)


def _uv_proj(lat, w_uv_t, *, layer):
    heads, m, r = lat.shape
    dv = w_uv_t.shape[-1]
    return _pcall(
        _uv_kernel, name="uv_proj", grid=(heads,),
        in_specs=[pl.BlockSpec((None, m, r), lambda h: (h, 0, 0)),
                  pl.BlockSpec((None, None, r, dv), lambda h: (layer, h, 0, 0))],
        out_specs=pl.BlockSpec((m, dv), lambda h: (0, h)),
        out_shape=jax.ShapeDtypeStruct((m, heads * dv), BF16),
        args=(lat, w_uv_t),
        temp_bytes=m * r * 2 + m * dv * 4,
    )


def _out_kernel(a_ref, gc_ref, wa_ref, wc_ref, x_ref, gate_ref, gpost_ref, o_ref):
    out = _dot(a_ref[...], wa_ref[...]) + _dot(gc_ref[...], wc_ref[...])
    o_ref[...] = _gated_residual(x_ref[...], gate_ref[...], _rms(out, gpost_ref[...]))


def _out_proj(attn, gconv, w_o, x2d, gate, g_post, *, layer, tm, mod_block, mod_index):
    m, d = x2d.shape
    ka, kc = attn.shape[1], gconv.shape[1]
    assert ka == kc and w_o.shape[1] == ka + kc
    return _pcall(
        _out_kernel, name="out_proj", grid=(m // tm,),
        in_specs=[
            pl.BlockSpec((tm, ka), lambda i: (i, 0)),
            pl.BlockSpec((tm, kc), lambda i: (i, 0)),
            pl.BlockSpec((None, ka, d), lambda i: (layer, 0, 0)),
            pl.BlockSpec((None, kc, d), lambda i: (layer, 1, 0)),
            pl.BlockSpec((tm, d), lambda i: (i, 0)),
            pl.BlockSpec(mod_block, mod_index),
            pl.BlockSpec((1, d), lambda i: (0, 0)),
        ],
        out_specs=pl.BlockSpec((tm, d), lambda i: (i, 0)),
        out_shape=jax.ShapeDtypeStruct((m, d), F32),
        args=(attn, gconv, w_o, w_o, x2d, gate, g_post),
        temp_bytes=tm * d * 4 * 3,
    )


def _mlp_kernel(x_ref, sh_ref, sc_ref, gate_ref, gpre_ref, gpost_ref, wu_ref, wd_ref, o_ref, h_sc, acc_sc):
    f = pl.program_id(1)
    last = pl.num_programs(1) - 1

    def ff(h):
        a = jnp.maximum(_dot(h, wu_ref[...]), 0.0)
        return _dot((a * a).astype(BF16), wd_ref[...])

    @pl.when(f == 0)
    def _():
        h = _modulate(x_ref[...], gpre_ref[...], sh_ref[...], sc_ref[...]).astype(BF16)
        h_sc[...] = h
        acc_sc[...] = ff(h)

    @pl.when(jnp.logical_and(f > 0, f < last))
    def _():
        acc_sc[...] += ff(h_sc[...])

    @pl.when(f == last)
    def _():
        acc = acc_sc[...] + ff(h_sc[...])
        o_ref[...] = _gated_residual(x_ref[...], gate_ref[...], _rms(acc, gpost_ref[...]))


def _mlp(x2d, shift, scale, gate, g_pre, g_post, w_up, w_down, *, layer, tm, tf, mod_block, mod_index):
    m, d = x2d.shape
    ff = w_up.shape[2]
    assert ff // tf >= 2
    mi = lambda i, f: mod_index(i)
    return _pcall(
        _mlp_kernel, name="mlp", grid=(m // tm, ff // tf),
        in_specs=[
            pl.BlockSpec((tm, d), lambda i, f: (i, 0)),
            pl.BlockSpec(mod_block, mi),
            pl.BlockSpec(mod_block, mi),
            pl.BlockSpec(mod_block, mi),
            pl.BlockSpec((1, d), lambda i, f: (0, 0)),
            pl.BlockSpec((1, d), lambda i, f: (0, 0)),
            pl.BlockSpec((None, d, tf), lambda i, f: (layer, 0, f)),
            pl.BlockSpec((None, tf, d), lambda i, f: (layer, f, 0)),
        ],
        out_specs=pl.BlockSpec((tm, d), lambda i, f: (i, 0)),
        out_shape=jax.ShapeDtypeStruct((m, d), F32),
        args=(x2d, shift, scale, gate, g_pre, g_post, w_up, w_down),
        scratch=[((tm, d), BF16), ((tm, d), F32)],
        temp_bytes=tm * tf * 4 * 3 + tm * d * 4 * 2,
    )


def _rope_tables(pos):
    half = 32
    inv = ROPE_THETA ** (-jnp.arange(half, dtype=F32) / half)
    ang = pos.astype(F32)[:, None] * inv[None, :]
    cos, sin = jnp.cos(ang), jnp.sin(ang)
    zero = jnp.zeros((pos.shape[0], LANE - 2 * half), F32)
    return (jnp.concatenate([cos, cos, zero], axis=-1),
            jnp.concatenate([-sin, sin, zero], axis=-1))


def kernel(x_prompt, x_sample, c_prompt, c_sample, cache_ckv, cache_kpe, state_conv, page_table, w_ada, b_ada,
           g_pre_mix, g_post_mix, w_in, g_q_lat, g_kv_lat, w_uq, w_uk, w_uv, conv_w, w_o, g_pre_mlp, g_post_mlp,
           w_up, w_down):
    batch, seq, d = x_prompt.shape
    nb, t_new, _ = x_sample.shape
    depth = w_in.shape[0]
    r_q, heads, qk = w_uq.shape[1:]
    r_kv, _, nope = w_uk.shape[1:]
    dv = w_uv.shape[-1]
    pe = qk - nope
    cdim = conv_w.shape[-1]
    past = page_table.shape[1] * cache_ckv.shape[2]
    q_scale = math.log2(math.e) / math.sqrt(qk)
    mp_rows, ms_rows = batch * seq, nb * t_new

    tm_p, tm_s = 512, 512
    tps = seq // tm_p

    n_c = batch + nb
    pad = (-n_c) % SUBLANE
    c_all = jnp.concatenate([c_prompt, c_sample, jnp.zeros((pad, d), F32)], axis=0)
    mod = _adaln(c_all, w_ada, b_ada)

    cos_p, sin_p = _rope_tables(jnp.arange(seq, dtype=jnp.int32))
    cos_s, sin_s = _rope_tables(past + jnp.arange(t_new, dtype=jnp.int32))
    cos_s = jnp.tile(cos_s, (tm_s // t_new, 1))
    sin_s = jnp.tile(sin_s, (tm_s // t_new, 1))

    mod_index_p = lambda i: (i // tps, 0, 0)
    mod_index_s = lambda i: (i, 0)
    rope_index_p = lambda i: (i % tps, 0)
    rope_index_s = lambda i: (0, 0)
    kw_p = dict(tm=tm_p, mod_block=(1, 1, d), mod_index=mod_index_p)
    kw_s = dict(tm=tm_s, mod_block=(tm_s // t_new, d), mod_index=mod_index_s)
    kw_gs = dict(tm=tm_s // 2, mod_block=(tm_s // 2 // t_new, d), mod_index=mod_index_s)

    cache_kpe_t = jnp.swapaxes(cache_kpe, 2, 3)

    s3 = r_q + r_kv + pe
    w_qkv = jnp.concatenate([w_in[:, :, :s3], jnp.zeros((depth, d, LANE - pe), F32)], axis=2).astype(BF16)
    w_gate = w_in[:, :, s3:].astype(BF16)
    w_nope = w_uq[..., :nope].reshape(depth, r_q, heads * nope).astype(BF16)
    w_pe = jnp.concatenate([w_uq[..., nope:], jnp.zeros((depth, r_q, heads, LANE - pe), F32)],
                           axis=-1).reshape(depth, r_q, heads * LANE).astype(BF16)
    w_uk_t = jnp.transpose(w_uk, (0, 2, 3, 1)).astype(BF16)
    w_uv_t = jnp.transpose(w_uv, (0, 2, 1, 3)).astype(BF16)
    w_o_b = w_o.astype(BF16)
    w_up_b = w_up.astype(BF16)
    w_down_b = w_down.astype(BF16)

    x_p = x_prompt.reshape(mp_rows, d)
    x_s = x_sample.reshape(ms_rows, d)
    outs = [[] for _ in range(6)]
    for l in range(depth):
        g_pre = g_pre_mix[l].reshape(1, d)
        g_post = g_post_mix[l].reshape(1, d)
        g_q = g_q_lat[l].reshape(1, r_q)
        g_kv = g_kv_lat[l].reshape(1, r_kv)
        g_pre2 = g_pre_mlp[l].reshape(1, d)
        g_post2 = g_post_mlp[l].reshape(1, d)

        mods_p = [mod[l, :batch, k * d:(k + 1) * d].reshape(batch, 1, d) for k in range(6)]
        mods_s = [mod[l, batch:batch + nb, k * d:(k + 1) * d] for k in range(6)]

        qa, qp, ckv_p, kpe_p, ckv_pb, kpe_pb = _qkv(
            x_p, mods_p[0], mods_p[1], g_pre, w_qkv, g_q, g_kv, w_nope, w_pe, w_uk_t, cos_p, sin_p,
            layer=l, rope_index=rope_index_p, q_dtype=BF16, qp_width=LANE, q_scale=q_scale, **kw_p)
        gconv_p, tail_p = _gate(x_p, mods_p[0], mods_p[1], g_pre, w_gate, conv_w, None, None,
                                layer=l, tc=512, seq_rows=seq, **kw_p)
        attn_p = _flash(qa, qp, ckv_pb, kpe_pb, w_uv_t, layer=l, batch=batch, seq=seq, tq=256, tk=512, groups=2)
        x_p = _out_proj(attn_p, gconv_p, w_o_b, x_p, mods_p[2], g_post, layer=l, **kw_p)
        x_p = _mlp(x_p, mods_p[3], mods_p[4], mods_p[5], g_pre2, g_post2, w_up_b, w_down_b,
                   layer=l, tf=1024, **kw_p)

        qa, qp, ckv_s, kpe_s, _, _ = _qkv(
            x_s, mods_s[0], mods_s[1], g_pre, w_qkv, g_q, g_kv, w_nope, w_pe, w_uk_t, cos_s, sin_s,
            layer=l, rope_index=rope_index_s, q_dtype=F32, qp_width=pe, q_scale=q_scale, **kw_s)
        prev0 = jnp.repeat(state_conv[l][:, 0, :], t_new, axis=0)
        prev1 = jnp.repeat(state_conv[l][:, 1, :], t_new, axis=0)
        gconv_s, u_s = _gate(x_s, mods_s[0], mods_s[1], g_pre, w_gate, conv_w, prev0, prev1,
                             layer=l, tc=512, seq_rows=t_new, **kw_gs)
        lat_s = _paged(qa, qp, cache_ckv, cache_kpe_t, l, page_table, ckv_s, kpe_s, pages=32, nbuf=3)
        attn_s = _uv_proj(lat_s, w_uv_t, layer=l)
        x_s = _out_proj(attn_s, gconv_s, w_o_b, x_s, mods_s[2], g_post, layer=l, **kw_s)
        x_s = _mlp(x_s, mods_s[3], mods_s[4], mods_s[5], g_pre2, g_post2, w_up_b, w_down_b,
                   layer=l, tf=1024, **kw_s)

        outs[0].append(ckv_p.reshape(batch, seq, r_kv))
        outs[1].append(kpe_p.reshape(batch, seq, pe))
        outs[2].append(tail_p[tps - 1::tps, SUBLANE - 2:, :])
        outs[3].append(ckv_s.reshape(nb, t_new, r_kv))
        outs[4].append(kpe_s.reshape(nb, t_new, pe))
        outs[5].append(u_s.reshape(nb, t_new, cdim)[:, t_new - 2:, :])

    return (x_p.reshape(batch, seq, d), x_s.reshape(nb, t_new, d),
            jnp.stack(outs[0]), jnp.stack(outs[1]), jnp.stack(outs[2]),
            jnp.stack(outs[3]), jnp.stack(outs[4]), jnp.stack(outs[5]))
```
